```python
import jax, jax.numpy as jnp
from jax import lax
import numpy as np

D_MODEL = 2048
BATCH = 1
SEQ = 16384
DEPTH = 1

N_HEADS = 16
HEAD_DIM = 128
ATTN_WIDTH = N_HEADS * HEAD_DIM
MOBA_BLOCK = 256
MOBA_TOPK = 3
Q_CHUNK = 64
CONV_WIDTH = 1024
CONV_KERNEL = 31
D_FF = 5632
LN_EPS = 1e-5
DN_ALPHA = (2 * DEPTH) ** 0.25
DN_BETA = (8 * DEPTH) ** -0.25
IN_SPLITS = [ATTN_WIDTH, ATTN_WIDTH, ATTN_WIDTH, 2 * CONV_WIDTH, D_MODEL, D_MODEL]
IN_COLS = sum(IN_SPLITS)

kernel_name = "moba_conformer_gated_hybrid_deepnorm"


def alibi_slopes():
    return jnp.asarray(np.array([2.0 ** (-8.0 * (h + 1) / N_HEADS) for h in range(N_HEADS)], dtype=np.float32))


def layer_norm(x, g, b):
    xf = x.astype(jnp.float32)
    mu = jnp.mean(xf, -1, keepdims=True)
    var = jnp.mean(jnp.square(xf - mu), -1, keepdims=True)
    return ((xf - mu) * lax.rsqrt(var + LN_EPS)).astype(x.dtype) * g + b


def swiglu(x, wg, wu, wd):
    return (jax.nn.silu(x @ wg) * (x @ wu)) @ wd


def moba_attention(q, k, v):
    B, H, S, Dh = q.shape
    L = MOBA_BLOCK
    nb = -(-S // L)
    pad = ((0, 0), (0, 0), (0, nb * L - S), (0, 0))
    kp = jnp.pad(k, pad)
    vp = jnp.pad(v, pad)
    k_blk = kp.reshape(B, H, nb, L, Dh)
    v_blk = vp.reshape(B, H, nb, L, Dh)
    k_mean = jnp.mean(k_blk.astype(jnp.float32), axis=3)
    q_blk = jnp.arange(S) // L
    gate = jnp.einsum('bhsd,bhnd->bhsn', q.astype(jnp.float32), k_mean)
    past = jnp.arange(nb)[None, :] < q_blk[:, None]
    gate = jnp.where(past, gate, -jnp.inf)
    n_sel = min(MOBA_TOPK, nb)
    _, sel_idx = lax.top_k(gate, n_sel)
    sel_valid = jnp.arange(n_sel)[None, :] < q_blk[:, None]

    scale = HEAD_DIM ** -0.5
    slopes = alibi_slopes()
    sl4 = slopes.reshape(1, H, 1, 1)
    sl5 = slopes.reshape(1, H, 1, 1, 1)
    bi = jnp.arange(B)[:, None, None, None]
    hi = jnp.arange(H)[None, :, None, None]

    def chunk(c):
        t0 = c * Q_CHUNK
        t = t0 + jnp.arange(Q_CHUNK)
        q_c = lax.dynamic_slice_in_dim(q, t0, Q_CHUNK, axis=2)
        idx_c = lax.dynamic_slice_in_dim(sel_idx, t0, Q_CHUNK, axis=2)
        val_c = lax.dynamic_slice_in_dim(sel_valid, t0, Q_CHUNK, axis=0)
        k_sel = k_blk[bi, hi, idx_c]
        v_sel = v_blk[bi, hi, idx_c]
        s_sel = idx_c[..., None] * L + jnp.arange(L)
        sc_sel = jnp.einsum('bhqd,bhqnld->bhqnl', q_c, k_sel).astype(jnp.float32) * scale
        sc_sel = sc_sel - sl5 * (t[:, None, None] - s_sel).astype(jnp.float32)
        sc_sel = jnp.where(val_c[:, :, None], sc_sel, -jnp.inf)
        own0 = (t0 // L) * L
        k_own = lax.dynamic_slice_in_dim(kp, own0, L, axis=2)
        v_own = lax.dynamic_slice_in_dim(vp, own0, L, axis=2)
        s_own = own0 + jnp.arange(L)
        sc_own = jnp.einsum('bhqd,bhld->bhql', q_c, k_own).astype(jnp.float32) * scale
        sc_own = sc_own - sl4 * (t[:, None] - s_own[None, :]).astype(jnp.float32)
        sc_own = jnp.where(s_own[None, :] <= t[:, None], sc_own, -jnp.inf)
        scores = jnp.concatenate([sc_sel.reshape(B, H, Q_CHUNK, n_sel * L), sc_own], axis=-1)
        p = jax.nn.softmax(scores, axis=-1)
        p_sel = p[..., :n_sel * L].reshape(B, H, Q_CHUNK, n_sel, L).astype(v.dtype)
        p_own = p[..., n_sel * L:].astype(v.dtype)
        return (jnp.einsum('bhqnl,bhqnld->bhqd', p_sel, v_sel)
                + jnp.einsum('bhql,bhld->bhqd', p_own, v_own))

    outs = lax.map(chunk, jnp.arange(S // Q_CHUNK))
    return outs.transpose(1, 2, 0, 3, 4).reshape(B, H, S, Dh)


def conformer_conv(u, w_dw, b_dw, ln_g, ln_b, w_pw2):
    a, g = jnp.split(u, 2, axis=-1)
    h = a * jax.nn.sigmoid(g)
    h = lax.conv_general_dilated(h, w_dw, window_strides=(1,), padding=[(CONV_KERNEL - 1, 0)],
                                 dimension_numbers=('NWC', 'WIO', 'NWC'),
                                 feature_group_count=CONV_WIDTH) + b_dw
    h = jax.nn.silu(layer_norm(h, ln_g, ln_b))
    return h @ w_pw2


def token_mixer(h, w_in, conv_dw, conv_db, conv_ln_g, conv_ln_b, w_conv_out, w_attn_out, w_out):
    B, S, _ = h.shape
    z = h @ w_in
    q, k, v, u, ga, gc = jnp.split(z, list(np.cumsum(IN_SPLITS)[:-1]), axis=-1)
    to_heads = lambda t: t.reshape(B, S, N_HEADS, HEAD_DIM).transpose(0, 2, 1, 3)
    o = moba_attention(to_heads(q), to_heads(k), to_heads(v))
    y_attn = o.transpose(0, 2, 1, 3).reshape(B, S, ATTN_WIDTH) @ w_attn_out
    y_conv = conformer_conv(u, conv_dw, conv_db, conv_ln_g, conv_ln_b, w_conv_out)
    merged = jax.nn.sigmoid(ga) * y_attn + jax.nn.sigmoid(gc) * y_conv
    return merged @ w_out


def setup_inputs(seed: int = 0) -> dict:
    key = jax.random.key(seed)
    ks = jax.random.split(key, 24)
    f32 = jnp.float32
    nrm = lambda k, shape, s: jax.random.normal(k, shape, f32) * s
    gain = lambda k, n: 1.0 + nrm(k, (DEPTH, n), 0.02)
    bias = lambda k, n: nrm(k, (DEPTH, n), 0.02)
    return {
        "x": jax.random.normal(ks[0], (BATCH, SEQ, D_MODEL), f32),
        "ln1_g": gain(ks[1], D_MODEL), "ln1_b": bias(ks[2], D_MODEL),
        "ffn1_wg": nrm(ks[3], (DEPTH, D_MODEL, D_FF), D_MODEL ** -0.5),
        "ffn1_wu": nrm(ks[4], (DEPTH, D_MODEL, D_FF), D_MODEL ** -0.5),
        "ffn1_wd": nrm(ks[5], (DEPTH, D_FF, D_MODEL), DN_BETA * D_FF ** -0.5),
        "w_in": nrm(ks[6], (DEPTH, D_MODEL, IN_COLS), D_MODEL ** -0.5),
        "conv_dw": nrm(ks[7], (DEPTH, CONV_KERNEL, 1, CONV_WIDTH), CONV_KERNEL ** -0.5),
        "conv_db": bias(ks[8], CONV_WIDTH),
        "conv_ln_g": gain(ks[9], CONV_WIDTH), "conv_ln_b": bias(ks[10], CONV_WIDTH),
        "w_conv_out": nrm(ks[11], (DEPTH, CONV_WIDTH, D_MODEL), DN_BETA * CONV_WIDTH ** -0.5),
        "w_attn_out": nrm(ks[12], (DEPTH, ATTN_WIDTH, D_MODEL), DN_BETA * ATTN_WIDTH ** -0.5),
        "w_out": nrm(ks[13], (DEPTH, D_MODEL, D_MODEL), DN_BETA * D_MODEL ** -0.5),
        "ln2_g": gain(ks[14], D_MODEL), "ln2_b": bias(ks[15], D_MODEL),
        "ffn2_wg": nrm(ks[16], (DEPTH, D_MODEL, D_FF), D_MODEL ** -0.5),
        "ffn2_wu": nrm(ks[17], (DEPTH, D_MODEL, D_FF), D_MODEL ** -0.5),
        "ffn2_wd": nrm(ks[18], (DEPTH, D_FF, D_MODEL), DN_BETA * D_FF ** -0.5),
        "ln3_g": gain(ks[19], D_MODEL), "ln3_b": bias(ks[20], D_MODEL),
    }


def reference(x, ln1_g, ln1_b, ffn1_wg, ffn1_wu, ffn1_wd, w_in, conv_dw, conv_db, conv_ln_g, conv_ln_b,
              w_conv_out, w_attn_out, w_out, ln2_g, ln2_b, ffn2_wg, ffn2_wu, ffn2_wd, ln3_g, ln3_b):
    h = x
    for l in range(DEPTH):
        h = layer_norm(DN_ALPHA * h + 0.5 * swiglu(h, ffn1_wg[l], ffn1_wu[l], ffn1_wd[l]), ln1_g[l], ln1_b[l])
        mix = token_mixer(h, w_in[l], conv_dw[l], conv_db[l], conv_ln_g[l], conv_ln_b[l],
                          w_conv_out[l], w_attn_out[l], w_out[l])
        h = layer_norm(DN_ALPHA * h + mix, ln2_g[l], ln2_b[l])
        h = layer_norm(DN_ALPHA * h + 0.5 * swiglu(h, ffn2_wg[l], ffn2_wu[l], ffn2_wd[l]), ln3_g[l], ln3_b[l])
    return h
```

```python
import functools

import numpy as np
import jax
import jax.numpy as jnp
from jax import lax
from jax.experimental import pallas as pl
from jax.experimental.pallas import tpu as pltpu

HEAD_DIM = 128
MOBA_BLOCK = 256
MOBA_TOPK = 3
CONV_KERNEL = 31
LN_EPS = 1e-5
DEPTH = 1
DN_ALPHA = (2 * DEPTH) ** 0.25
ALIBI_MAX_LOG2 = 8.0

V7X_VMEM_BYTES = 64 * 1024 * 1024
V7X_LANES = 128
V7X_SUBLANES = 8
CONV_HALO = 32
MASK_NEG = -1e30

f32 = jnp.float32
bf16 = jnp.bfloat16


def _vmem_limit(block_bytes):
    return int(min(V7X_VMEM_BYTES - 4 * 1024 * 1024, block_bytes + 16 * 1024 * 1024))


def _tile(dim, pref):
    if dim <= pref:
        return dim
    for t in range(pref, 0, -V7X_LANES):
        if dim % t == 0:
            return t
    raise ValueError((dim, pref))


def _layer_norm(y, g, b):
    mu = jnp.mean(y, axis=-1, keepdims=True)
    yc = y - mu
    var = jnp.mean(yc * yc, axis=-1, keepdims=True)
    return yc * lax.rsqrt(var + LN_EPS) * g + b


def _ffn_ln_kernel(x_ref, wg_ref, wu_ref, wd_ref, g_ref, b_ref, o_ref, ob_ref, xb_ref, acc_ref):
    f = pl.program_id(1)

    @pl.when(f == 0)
    def _():
        xb_ref[...] = x_ref[...].astype(bf16)

    xb = xb_ref[...]
    gate = jnp.dot(xb, wg_ref[...], preferred_element_type=f32)
    up = jnp.dot(xb, wu_ref[...], preferred_element_type=f32)
    act = (gate * jax.nn.sigmoid(gate) * up).astype(bf16)
    down = jnp.dot(act, wd_ref[...], preferred_element_type=f32)

    @pl.when(f == 0)
    def _():
        acc_ref[...] = down

    @pl.when(f > 0)
    def _():
        acc_ref[...] += down

    @pl.when(f == pl.num_programs(1) - 1)
    def _():
        y = DN_ALPHA * x_ref[...] + 0.5 * acc_ref[...]
        out = _layer_norm(y, g_ref[...], b_ref[...])
        o_ref[...] = out
        ob_ref[...] = out.astype(bf16)


def _ffn_ln(x, wg, wu, wd, g, b):
    s, d = x.shape
    ff = wg.shape[1]
    tm = _tile(s, 512)
    tf = _tile(ff, 512)
    blocks = 2 * (tm * d * 4 + 2 * d * tf * 2 + tf * d * 2 + tm * d * 4 + tm * d * 2)
    scratch = tm * d * 2 + tm * d * 4
    return pl.pallas_call(
        _ffn_ln_kernel,
        grid=(s // tm, ff // tf),
        in_specs=[
            pl.BlockSpec((tm, d), lambda i, j: (i, 0)),
            pl.BlockSpec((d, tf), lambda i, j: (0, j)),
            pl.BlockSpec((d, tf), lambda i, j: (0, j)),
            pl.BlockSpec((tf, d), lambda i, j: (j, 0)),
            pl.BlockSpec((1, d), lambda i, j: (0, 0)),
            pl.BlockSpec((1, d), lambda i, j: (0, 0)),
        ],
        out_specs=[
            pl.BlockSpec((tm, d), lambda i, j: (i, 0)),
            pl.BlockSpec((tm, d), lambda i, j: (i, 0)),
        ],
        out_shape=[jax.ShapeDtypeStruct((s, d), f32), jax.ShapeDtypeStruct((s, d), bf16)],
        scratch_shapes=[pltpu.VMEM((tm, d), bf16), pltpu.VMEM((tm, d), f32)],
        compiler_params=pltpu.CompilerParams(
            dimension_semantics=("arbitrary", "arbitrary"),
            vmem_limit_bytes=_vmem_limit(blocks + scratch)),
        name="ffn_ln",
    )(x, wg, wu, wd, g, b)


def _proj_kernel(x_ref, w_ref, o_ref, *, sigmoid):
    y = jnp.dot(x_ref[...], w_ref[...], preferred_element_type=f32)
    if sigmoid:
        y = jax.nn.sigmoid(y)
    o_ref[...] = y.astype(o_ref.dtype)


def _proj(x, w, *, sigmoid, name):
    s, d = x.shape
    n = w.shape[1]
    tm = _tile(s, 1024)
    tn = _tile(n, 512)
    blocks = 2 * (tm * d * 2 + d * tn * 2 + tm * tn * 2)
    return pl.pallas_call(
        functools.partial(_proj_kernel, sigmoid=sigmoid),
        grid=(s // tm, n // tn),
        in_specs=[
            pl.BlockSpec((tm, d), lambda i, j: (i, 0)),
            pl.BlockSpec((d, tn), lambda i, j: (0, j)),
        ],
        out_specs=pl.BlockSpec((tm, tn), lambda i, j: (i, j)),
        out_shape=jax.ShapeDtypeStruct((s, n), bf16),
        compiler_params=pltpu.CompilerParams(
            dimension_semantics=("arbitrary", "arbitrary"),
            vmem_limit_bytes=_vmem_limit(blocks)),
        name=name,
    )(x, w)


def _glu_proj_kernel(x_ref, wa_ref, wg_ref, o_ref):
    x = x_ref[...]
    a = jnp.dot(x, wa_ref[...], preferred_element_type=f32)
    g = jnp.dot(x, wg_ref[...], preferred_element_type=f32)
    o_ref[...] = a * jax.nn.sigmoid(g)


def _glu_proj(x, wa, wg):
    s, d = x.shape
    n = wa.shape[1]
    tm = _tile(s, 1024)
    tn = _tile(n, 512)
    blocks = 2 * (tm * d * 2 + 2 * d * tn * 2 + tm * tn * 4)
    return pl.pallas_call(
        _glu_proj_kernel,
        grid=(s // tm, n // tn),
        in_specs=[
            pl.BlockSpec((tm, d), lambda i, j: (i, 0)),
            pl.BlockSpec((d, tn), lambda i, j: (0, j)),
            pl.BlockSpec((d, tn), lambda i, j: (0, j)),
        ],
        out_specs=pl.BlockSpec((tm, tn), lambda i, j: (i, j)),
        out_shape=jax.ShapeDtypeStruct((s, n), f32),
        compiler_params=pltpu.CompilerParams(
            dimension_semantics=("arbitrary", "arbitrary"),
            vmem_limit_bytes=_vmem_limit(blocks)),
        name="glu_proj",
    )(x, wa, wg)


_NT = (((1,), (1,)), ((), ()))


def _moba_kernel(q_ref, k_ref, v_ref, slope_ref, o_ref, kaug_ref, kmean_ref, *, nb, scale):
    qb = pl.program_id(1)
    blk = MOBA_BLOCK

    @pl.when(qb == 0)
    def _build_head():
        lane = lax.broadcasted_iota(jnp.int32, (blk, V7X_LANES), 1)

        def body(j, carry):
            r0 = pl.multiple_of(j * blk, blk)
            kj = k_ref[pl.ds(r0, blk), :]
            onehot = jnp.where(lane == j, 1.0, 0.0).astype(bf16)
            kaug_ref[pl.ds(r0, blk), :] = jnp.concatenate([kj, onehot], axis=1)
            kmean_ref[pl.ds(j, 1), :] = jnp.mean(kj.astype(f32), axis=0, keepdims=True)
            return carry

        lax.fori_loop(0, nb, body, 0)

    q = q_ref[...]
    gate = lax.dot_general(q, kmean_ref[...].astype(bf16), _NT, preferred_element_type=f32)
    col = lax.broadcasted_iota(jnp.int32, (blk, nb), 1)
    neg_inf = f32(-jnp.inf)
    gate = jnp.where(col < qb, gate, neg_inf)
    keep = col == qb
    for r in range(min(MOBA_TOPK, nb)):
        top = jnp.max(gate, axis=1, keepdims=True)
        idx = jnp.min(jnp.where(gate == top, col, nb), axis=1, keepdims=True)
        hit = col == idx
        keep = jnp.logical_or(keep, jnp.logical_and(hit, r < qb))
        gate = jnp.where(hit, neg_inf, gate)
    maskbias = jnp.where(keep, 0.0, MASK_NEG).astype(bf16)
    pad = jnp.zeros((blk, V7X_LANES - nb), bf16)
    qs = (q.astype(f32) * scale).astype(bf16)
    q_aug = jnp.concatenate([qs, maskbias, pad], axis=1)

    slope = slope_ref[0]
    cpos = lax.broadcasted_iota(jnp.int32, (1, blk), 1).astype(f32)

    d0 = pl.multiple_of(qb * blk, blk)
    s = lax.dot_general(q_aug, kaug_ref[pl.ds(d0, blk), :], _NT, preferred_element_type=f32)
    s = s + slope * cpos
    rr = lax.broadcasted_iota(jnp.int32, (blk, blk), 0)
    cc = lax.broadcasted_iota(jnp.int32, (blk, blk), 1)
    s = jnp.where(cc <= rr, s, MASK_NEG)
    m0 = jnp.max(s, axis=1, keepdims=True)
    p = jnp.exp(s - m0)
    l0 = jnp.sum(p, axis=1, keepdims=True)
    acc0 = jnp.dot(p.astype(bf16), v_ref[pl.ds(d0, blk), :], preferred_element_type=f32)

    def body(j, carry):
        m, l, acc = carry
        r0 = pl.multiple_of(j * blk, blk)
        sj = lax.dot_general(q_aug, kaug_ref[pl.ds(r0, blk), :], _NT, preferred_element_type=f32)
        sj = sj + slope * (cpos - ((qb - j) * blk).astype(f32))
        m_new = jnp.maximum(m, jnp.max(sj, axis=1, keepdims=True))
        alpha = jnp.exp(m - m_new)
        pj = jnp.exp(sj - m_new)
        l = alpha * l + jnp.sum(pj, axis=1, keepdims=True)
        acc = alpha * acc + jnp.dot(pj.astype(bf16), v_ref[pl.ds(r0, blk), :],
                                    preferred_element_type=f32)
        return m_new, l, acc

    _, l, acc = lax.fori_loop(0, qb, body, (m0, l0, acc0))
    o_ref[...] = (acc / l).astype(o_ref.dtype)


def _moba_attention(qkv, n_heads):
    s = qkv.shape[0]
    blk = MOBA_BLOCK
    assert s % blk == 0
    nb = s // blk
    assert nb <= V7X_LANES
    slopes = np.array([2.0 ** (-ALIBI_MAX_LOG2 * (h + 1) / n_heads) for h in range(n_heads)], np.float32)
    slope_rows = jnp.asarray(np.broadcast_to(slopes[:, None, None], (n_heads, 1, blk)).copy())
    blocks = 2 * (blk * HEAD_DIM * 2 + 2 * s * HEAD_DIM * 2 + blk * 4 + blk * HEAD_DIM * 2)
    scratch = s * 2 * HEAD_DIM * 2 + max(nb, V7X_SUBLANES) * HEAD_DIM * 4
    return pl.pallas_call(
        functools.partial(_moba_kernel, nb=nb, scale=HEAD_DIM ** -0.5),
        grid=(n_heads, nb),
        in_specs=[
            pl.BlockSpec((blk, HEAD_DIM), lambda h, i: (i, h)),
            pl.BlockSpec((s, HEAD_DIM), lambda h, i: (0, n_heads + h)),
            pl.BlockSpec((s, HEAD_DIM), lambda h, i: (0, 2 * n_heads + h)),
            pl.BlockSpec((1, 1, blk), lambda h, i: (h, 0, 0)),
        ],
        out_specs=pl.BlockSpec((blk, HEAD_DIM), lambda h, i: (i, h)),
        out_shape=jax.ShapeDtypeStruct((s, n_heads * HEAD_DIM), bf16),
        scratch_shapes=[pltpu.VMEM((s, 2 * HEAD_DIM), bf16), pltpu.VMEM((nb, HEAD_DIM), f32)],
        compiler_params=pltpu.CompilerParams(
            dimension_semantics=("arbitrary", "arbitrary"),
            vmem_limit_bytes=_vmem_limit(blocks + scratch)),
        name="moba_attn",
    )(qkv, qkv, qkv, slope_rows)


def _conv_ln_kernel(prev_ref, cur_ref, w_ref, cb_ref, g_ref, b_ref, o_ref, ext_ref, y_ref, *, tm, lane_chunk, row_chunk):
    i = pl.program_id(0)
    c = cur_ref.shape[1]
    ext_ref[0:CONV_HALO, :] = jnp.where(i > 0, prev_ref[...], 0.0)
    ext_ref[CONV_HALO:CONV_HALO + tm, :] = cur_ref[...]
    first = CONV_HALO - (CONV_KERNEL - 1)
    for r0 in range(0, tm, row_chunk):
        for c0 in range(0, c, lane_chunk):
            acc = jnp.zeros((row_chunk, lane_chunk), f32)
            for k in range(CONV_KERNEL):
                acc = acc + (w_ref[k:k + 1, c0:c0 + lane_chunk]
                             * ext_ref[r0 + first + k:r0 + first + k + row_chunk, c0:c0 + lane_chunk])
            y_ref[r0:r0 + row_chunk, c0:c0 + lane_chunk] = acc + cb_ref[:, c0:c0 + lane_chunk]
    y = _layer_norm(y_ref[...], g_ref[...], b_ref[...])
    o_ref[...] = (y * jax.nn.sigmoid(y)).astype(o_ref.dtype)


def _conv_ln(hc, w, cb, g, b):
    s, c = hc.shape
    tm = _tile(s, 256)
    assert tm % CONV_HALO == 0 and CONV_HALO >= CONV_KERNEL - 1
    per = tm // CONV_HALO
    blocks = 2 * (CONV_HALO * c * 4 + tm * c * 4 + CONV_KERNEL * c * 4 + tm * c * 2)
    scratch = (tm + CONV_HALO) * c * 4 + tm * c * 4
    return pl.pallas_call(
        functools.partial(_conv_ln_kernel, tm=tm, lane_chunk=_tile(c, 512), row_chunk=_tile(tm, 32)),
        grid=(s // tm,),
        in_specs=[
            pl.BlockSpec((CONV_HALO, c), lambda i: (jnp.maximum(i * per - 1, 0), 0)),
            pl.BlockSpec((tm, c), lambda i: (i, 0)),
            pl.BlockSpec((CONV_KERNEL, c), lambda i: (0, 0)),
            pl.BlockSpec((1, c), lambda i: (0, 0)),
            pl.BlockSpec((1, c), lambda i: (0, 0)),
            pl.BlockSpec((1, c), lambda i: (0, 0)),
        ],
        out_specs=pl.BlockSpec((tm, c), lambda i: (i, 0)),
        out_shape=jax.ShapeDtypeStruct((s, c), bf16),
        scratch_shapes=[pltpu.VMEM((tm + CONV_HALO, c), f32), pltpu.VMEM((tm, c), f32)],
        compiler_params=pltpu.CompilerParams(
            dimension_semantics=("arbitrary",),
            vmem_limit_bytes=_vmem_limit(blocks + scratch)),
        name="conv_ln",
    )(hc, hc, w, cb, g, b)


def _merge_kernel(o_ref, hc_ref, ga_ref, gc_ref, wa_ref, wc_ref, m_ref):
    ya = jnp.dot(o_ref[...], wa_ref[...], preferred_element_type=f32)
    yc = jnp.dot(hc_ref[...], wc_ref[...], preferred_element_type=f32)
    m_ref[...] = (ga_ref[...].astype(f32) * ya + gc_ref[...].astype(f32) * yc).astype(m_ref.dtype)


def _merge(o, hc, gates, wa, wc):
    s, da = o.shape
    dc = hc.shape[1]
    d = wa.shape[1]
    tm = _tile(s, 1024)
    tn = _tile(d, 512)
    nj = d // tn
    blocks = 2 * (tm * da * 2 + tm * dc * 2 + 2 * tm * tn * 2 + da * tn * 2 + dc * tn * 2 + tm * tn * 2)
    return pl.pallas_call(
        _merge_kernel,
        grid=(s // tm, nj),
        in_specs=[
            pl.BlockSpec((tm, da), lambda i, j: (i, 0)),
            pl.BlockSpec((tm, dc), lambda i, j: (i, 0)),
            pl.BlockSpec((tm, tn), lambda i, j: (i, j)),
            pl.BlockSpec((tm, tn), lambda i, j: (i, nj + j)),
            pl.BlockSpec((da, tn), lambda i, j: (0, j)),
            pl.BlockSpec((dc, tn), lambda i, j: (0, j)),
        ],
        out_specs=pl.BlockSpec((tm, tn), lambda i, j: (i, j)),
        out_shape=jax.ShapeDtypeStruct((s, d), bf16),
        compiler_params=pltpu.CompilerParams(
            dimension_semantics=("arbitrary", "arbitrary"),
            vmem_limit_bytes=_vmem_limit(blocks)),
        name="merge",
    )(o, hc, gates, gates, wa, wc)


def _out_ln_kernel(m_ref, h_ref, w_ref, g_ref, b_ref, o_ref):
    mix = jnp.dot(m_ref[...], w_ref[...], preferred_element_type=f32)
    o_ref[...] = _layer_norm(DN_ALPHA * h_ref[...] + mix, g_ref[...], b_ref[...])


def _out_ln(m, h, w, g, b):
    s, d = h.shape
    tm = _tile(s, 512)
    blocks = 2 * (tm * d * 2 + tm * d * 4 + d * d * 2 + tm * d * 4)
    return pl.pallas_call(
        _out_ln_kernel,
        grid=(s // tm,),
        in_specs=[
            pl.BlockSpec((tm, d), lambda i: (i, 0)),
            pl.BlockSpec((tm, d), lambda i: (i, 0)),
            pl.BlockSpec((d, d), lambda i: (0, 0)),
            pl.BlockSpec((1, d), lambda i: (0, 0)),
            pl.BlockSpec((1, d), lambda i: (0, 0)),
        ],
        out_specs=pl.BlockSpec((tm, d), lambda i: (i, 0)),
        out_shape=jax.ShapeDtypeStruct((s, d), f32),
        compiler_params=pltpu.CompilerParams(
            dimension_semantics=("arbitrary",),
            vmem_limit_bytes=_vmem_limit(blocks)),
        name="out_ln",
    )(m, h, w, g, b)


def kernel(x, ln1_g, ln1_b, ffn1_wg, ffn1_wu, ffn1_wd, w_in, conv_dw, conv_db, conv_ln_g, conv_ln_b, w_conv_out, w_attn_out, w_out, ln2_g, ln2_b, ffn2_wg, ffn2_wu, ffn2_wd, ln3_g, ln3_b):
    batch, seq, d_model = x.shape
    depth = ffn1_wg.shape[0]
    assert depth == DEPTH
    attn_w = w_attn_out.shape[1]
    conv_w = w_conv_out.shape[1]
    n_heads = attn_w // HEAD_DIM
    c0, c1, c2 = 3 * attn_w, 3 * attn_w + conv_w, 3 * attn_w + 2 * conv_w
    row = lambda v: v.reshape(1, -1)

    outs = []
    for bi in range(batch):
        h = x[bi]
        for l in range(depth):
            h, hb = _ffn_ln(h, ffn1_wg[l].astype(bf16), ffn1_wu[l].astype(bf16), ffn1_wd[l].astype(bf16),
                            row(ln1_g[l]), row(ln1_b[l]))
            w = w_in[l]
            qkv = _proj(hb, w[:, :c0].astype(bf16), sigmoid=False, name="qkv_proj")
            glu = _glu_proj(hb, w[:, c0:c1].astype(bf16), w[:, c1:c2].astype(bf16))
            gates = _proj(hb, w[:, c2:].astype(bf16), sigmoid=True, name="gate_proj")
            o = _moba_attention(qkv, n_heads)
            hc = _conv_ln(glu, conv_dw[l].reshape(CONV_KERNEL, conv_w), row(conv_db[l]),
                          row(conv_ln_g[l]), row(conv_ln_b[l]))
            merged = _merge(o, hc, gates, w_attn_out[l].astype(bf16), w_conv_out[l].astype(bf16))
            h = _out_ln(merged, h, w_out[l].astype(bf16), row(ln2_g[l]), row(ln2_b[l]))
            h, _ = _ffn_ln(h, ffn2_wg[l].astype(bf16), ffn2_wu[l].astype(bf16), ffn2_wd[l].astype(bf16),
                           row(ln3_g[l]), row(ln3_b[l]))
        outs.append(h)
    return outs[0][None] if batch == 1 else jnp.stack(outs, axis=0)
```

```python
import functools

import numpy as np
import jax
import jax.numpy as jnp
from jax import lax
from jax.experimental import pallas as pl
from jax.experimental.pallas import tpu as pltpu

HEAD_DIM = 128
MOBA_BLOCK = 256
MOBA_TOPK = 3
MOBA_CHUNK = 8
CONV_KERNEL = 31
LN_EPS = 1e-5
DEPTH = 1
DN_ALPHA = (2 * DEPTH) ** 0.25
ALIBI_MAX_LOG2 = 8.0
LOG2_E = 1.4426950408889634

V7X_VMEM_BYTES = 64 * 1024 * 1024
V7X_LANES = 128
V7X_SUBLANES = 8
CONV_HALO = 32
MASK_NEG = -1e30

f32 = jnp.float32
bf16 = jnp.bfloat16


def _vmem_limit(block_bytes):
    return int(min(V7X_VMEM_BYTES - 4 * 1024 * 1024, block_bytes + 16 * 1024 * 1024))


def _tile(dim, pref):
    if dim <= pref:
        return dim
    for t in range(pref, 0, -V7X_LANES):
        if dim % t == 0:
            return t
    raise ValueError((dim, pref))


def _layer_norm(y, g, b):
    mu = jnp.mean(y, axis=-1, keepdims=True)
    yc = y - mu
    var = jnp.mean(yc * yc, axis=-1, keepdims=True)
    return yc * lax.rsqrt(var + LN_EPS) * g + b


def _ffn_ln_kernel(x_ref, wg_ref, wu_ref, wd_ref, g_ref, b_ref, o_ref, ob_ref, xb_ref, acc_ref):
    f = pl.program_id(1)

    @pl.when(f == 0)
    def _():
        xb_ref[...] = x_ref[...].astype(bf16)

    xb = xb_ref[...]
    gate = jnp.dot(xb, wg_ref[...], preferred_element_type=f32)
    up = jnp.dot(xb, wu_ref[...], preferred_element_type=f32)
    act = (gate * jax.nn.sigmoid(gate) * up).astype(bf16)
    down = jnp.dot(act, wd_ref[...], preferred_element_type=f32)

    @pl.when(f == 0)
    def _():
        acc_ref[...] = down

    @pl.when(f > 0)
    def _():
        acc_ref[...] += down

    @pl.when(f == pl.num_programs(1) - 1)
    def _():
        y = DN_ALPHA * x_ref[...] + 0.5 * acc_ref[...]
        out = _layer_norm(y, g_ref[...], b_ref[...])
        o_ref[...] = out
        ob_ref[...] = out.astype(bf16)


def _ffn_ln(x, wg, wu, wd, g, b):
    s, d = x.shape
    ff = wg.shape[1]
    tm = _tile(s, 512)
    tf = _tile(ff, 512)
    blocks = 2 * (tm * d * 4 + 2 * d * tf * 2 + tf * d * 2 + tm * d * 4 + tm * d * 2)
    scratch = tm * d * 2 + tm * d * 4
    return pl.pallas_call(
        _ffn_ln_kernel,
        grid=(s // tm, ff // tf),
        in_specs=[
            pl.BlockSpec((tm, d), lambda i, j: (i, 0)),
            pl.BlockSpec((d, tf), lambda i, j: (0, j)),
            pl.BlockSpec((d, tf), lambda i, j: (0, j)),
            pl.BlockSpec((tf, d), lambda i, j: (j, 0)),
            pl.BlockSpec((1, d), lambda i, j: (0, 0)),
            pl.BlockSpec((1, d), lambda i, j: (0, 0)),
        ],
        out_specs=[
            pl.BlockSpec((tm, d), lambda i, j: (i, 0)),
            pl.BlockSpec((tm, d), lambda i, j: (i, 0)),
        ],
        out_shape=[jax.ShapeDtypeStruct((s, d), f32), jax.ShapeDtypeStruct((s, d), bf16)],
        scratch_shapes=[pltpu.VMEM((tm, d), bf16), pltpu.VMEM((tm, d), f32)],
        compiler_params=pltpu.CompilerParams(
            dimension_semantics=("arbitrary", "arbitrary"),
            vmem_limit_bytes=_vmem_limit(blocks + scratch)),
        name="ffn_ln",
    )(x, wg, wu, wd, g, b)


def _proj_kernel(x_ref, w_ref, o_ref, *, sigmoid):
    y = jnp.dot(x_ref[...], w_ref[...], preferred_element_type=f32)
    if sigmoid:
        y = jax.nn.sigmoid(y)
    o_ref[...] = y.astype(o_ref.dtype)


def _proj(x, w, *, sigmoid, name):
    s, d = x.shape
    n = w.shape[1]
    tm = _tile(s, 1024)
    tn = _tile(n, 512)
    blocks = 2 * (tm * d * 2 + d * tn * 2 + tm * tn * 2)
    return pl.pallas_call(
        functools.partial(_proj_kernel, sigmoid=sigmoid),
        grid=(s // tm, n // tn),
        in_specs=[
            pl.BlockSpec((tm, d), lambda i, j: (i, 0)),
            pl.BlockSpec((d, tn), lambda i, j: (0, j)),
        ],
        out_specs=pl.BlockSpec((tm, tn), lambda i, j: (i, j)),
        out_shape=jax.ShapeDtypeStruct((s, n), bf16),
        compiler_params=pltpu.CompilerParams(
            dimension_semantics=("arbitrary", "arbitrary"),
            vmem_limit_bytes=_vmem_limit(blocks)),
        name=name,
    )(x, w)


def _glu_proj_kernel(x_ref, wa_ref, wg_ref, o_ref):
    x = x_ref[...]
    a = jnp.dot(x, wa_ref[...], preferred_element_type=f32)
    g = jnp.dot(x, wg_ref[...], preferred_element_type=f32)
    o_ref[...] = a * jax.nn.sigmoid(g)


def _glu_proj(x, wa, wg):
    s, d = x.shape
    n = wa.shape[1]
    tm = _tile(s, 1024)
    tn = _tile(n, 512)
    blocks = 2 * (tm * d * 2 + 2 * d * tn * 2 + tm * tn * 4)
    return pl.pallas_call(
        _glu_proj_kernel,
        grid=(s // tm, n // tn),
        in_specs=[
            pl.BlockSpec((tm, d), lambda i, j: (i, 0)),
            pl.BlockSpec((d, tn), lambda i, j: (0, j)),
            pl.BlockSpec((d, tn), lambda i, j: (0, j)),
        ],
        out_specs=pl.BlockSpec((tm, tn), lambda i, j: (i, j)),
        out_shape=jax.ShapeDtypeStruct((s, n), f32),
        compiler_params=pltpu.CompilerParams(
            dimension_semantics=("arbitrary", "arbitrary"),
            vmem_limit_bytes=_vmem_limit(blocks)),
        name="glu_proj",
    )(x, wa, wg)


_NT = (((1,), (1,)), ((), ()))


def _moba_kernel(q_ref, k_ref, v_ref, slope_ref, o_ref, kaug_ref, kmean_ref, qaug_ref, s_ref, *, nb, chunk, scale):
    qb = pl.program_id(1)
    blk = MOBA_BLOCK
    span = chunk * blk

    @pl.when(qb == 0)
    def _build_head():
        lane = lax.broadcasted_iota(jnp.int32, (blk, V7X_LANES), 1)
        kmean_ref[...] = jnp.zeros_like(kmean_ref)

        def body(j, carry):
            r0 = pl.multiple_of(j * blk, blk)
            kj = k_ref[pl.ds(r0, blk), :]
            onehot = jnp.where(lane == j, 1.0, 0.0).astype(bf16)
            kaug_ref[pl.ds(r0, blk), :] = jnp.concatenate([kj, onehot], axis=1)
            kmean_ref[pl.ds(j, 1), :] = jnp.mean(kj.astype(f32), axis=0, keepdims=True)
            return carry

        lax.fori_loop(0, nb, body, 0)

    q = q_ref[...]
    gate = lax.dot_general(kmean_ref[...].astype(bf16), q, _NT, preferred_element_type=f32)
    row = lax.broadcasted_iota(jnp.int32, (V7X_LANES, blk), 0)
    neg_inf = f32(-jnp.inf)
    gate = jnp.where(row < qb, gate, neg_inf)
    keep = jnp.zeros((V7X_LANES, blk), jnp.bool_)
    for r in range(min(MOBA_TOPK, nb)):
        top = jnp.max(gate, axis=0, keepdims=True)
        idx = jnp.min(jnp.where(gate == top, row, V7X_LANES), axis=0, keepdims=True)
        hit = row == idx
        keep = jnp.logical_or(keep, jnp.logical_and(hit, r < qb))
        gate = jnp.where(hit, neg_inf, gate)
    maskbias = jnp.where(keep, 0.0, MASK_NEG).T.astype(bf16)
    qs = (q.astype(f32) * scale).astype(bf16)
    qaug_ref[...] = jnp.concatenate([qs, maskbias], axis=1)

    slope = slope_ref[0]
    cpos = lax.broadcasted_iota(jnp.int32, (1, blk), 1).astype(f32)
    half = blk // 2

    def lane_fold(x, op):
        return op(x[:, :half], x[:, half:])

    d0 = pl.multiple_of(qb * blk, blk)
    s = lax.dot_general(qs, k_ref[pl.ds(d0, blk), :], _NT, preferred_element_type=f32)
    s = s + slope * cpos
    rr = lax.broadcasted_iota(jnp.int32, (blk, blk), 0)
    cc = lax.broadcasted_iota(jnp.int32, (blk, blk), 1)
    s = jnp.where(cc <= rr, s, MASK_NEG)
    m0 = jnp.max(s, axis=1, keepdims=True)
    p = jnp.exp2(s - m0)
    l0 = lane_fold(p, jnp.add)
    acc0 = jnp.dot(p.astype(bf16), v_ref[pl.ds(d0, blk), :], preferred_element_type=f32)

    def score_pass(c):
        off = (qb * blk - c * span).astype(f32)
        mx = None
        for t in range(chunk):
            r0 = pl.multiple_of(c * span + t * blk, blk)
            st = lax.dot_general(qaug_ref[...], kaug_ref[pl.ds(r0, blk), :], _NT,
                                 preferred_element_type=f32)
            st = st + slope * (cpos + (t * blk) - off)
            s_ref[:, t * blk:(t + 1) * blk] = st
            e = lane_fold(st, jnp.maximum)
            mx = e if mx is None else jnp.maximum(mx, e)
        return jnp.max(mx, axis=1, keepdims=True)

    def value_pass(c, m_new):
        ls = jnp.zeros((blk, half), f32)
        pv = jnp.zeros((blk, HEAD_DIM), f32)
        for t in range(chunk):
            r0 = pl.multiple_of(c * span + t * blk, blk)
            pt = jnp.exp2(s_ref[:, t * blk:(t + 1) * blk] - m_new)
            ls = ls + lane_fold(pt, jnp.add)
            pv = pv + jnp.dot(pt.astype(bf16), v_ref[pl.ds(r0, blk), :], preferred_element_type=f32)
        return ls, pv

    n_chunks = lax.div(qb + (chunk - 1), chunk)
    last_chunk = nb // chunk - 1

    def body(c, carry):
        m, l, acc, mx_c = carry
        m_new = jnp.maximum(m, mx_c)
        alpha = jnp.exp2(m - m_new)
        ls, pv = value_pass(c, m_new)
        mx_next = score_pass(jnp.minimum(c + 1, last_chunk))
        return m_new, alpha * l + ls, alpha * acc + pv, mx_next

    _, l, acc, _ = lax.fori_loop(0, n_chunks, body, (m0, l0, acc0, score_pass(0)))
    o_ref[...] = (acc / jnp.sum(l, axis=1, keepdims=True)).astype(o_ref.dtype)


def _moba_attention(qkv, n_heads):
    s = qkv.shape[0]
    blk = MOBA_BLOCK
    assert s % blk == 0
    nb = s // blk
    assert nb <= V7X_LANES
    chunk = MOBA_CHUNK if nb % MOBA_CHUNK == 0 else 1
    span = chunk * blk
    slopes = np.array([2.0 ** (-ALIBI_MAX_LOG2 * (h + 1) / n_heads) for h in range(n_heads)], np.float32)
    slopes = (slopes.astype(np.float64) * LOG2_E).astype(np.float32)
    slope_rows = jnp.asarray(np.broadcast_to(slopes[:, None, None], (n_heads, 1, blk)).copy())
    blocks = 2 * (blk * HEAD_DIM * 2 + 2 * s * HEAD_DIM * 2 + blk * 4 + blk * HEAD_DIM * 2)
    scratch = s * 2 * HEAD_DIM * 2 + V7X_LANES * HEAD_DIM * 4 + blk * 2 * HEAD_DIM * 2 + blk * span * 4
    return pl.pallas_call(
        functools.partial(_moba_kernel, nb=nb, chunk=chunk, scale=HEAD_DIM ** -0.5 * LOG2_E),
        grid=(n_heads, nb),
        in_specs=[
            pl.BlockSpec((blk, HEAD_DIM), lambda h, i: (i, h)),
            pl.BlockSpec((s, HEAD_DIM), lambda h, i: (0, n_heads + h)),
            pl.BlockSpec((s, HEAD_DIM), lambda h, i: (0, 2 * n_heads + h)),
            pl.BlockSpec((1, 1, blk), lambda h, i: (h, 0, 0)),
        ],
        out_specs=pl.BlockSpec((blk, HEAD_DIM), lambda h, i: (i, h)),
        out_shape=jax.ShapeDtypeStruct((s, n_heads * HEAD_DIM), bf16),
        scratch_shapes=[pltpu.VMEM((s, 2 * HEAD_DIM), bf16), pltpu.VMEM((V7X_LANES, HEAD_DIM), f32),
                        pltpu.VMEM((blk, 2 * HEAD_DIM), bf16), pltpu.VMEM((blk, span), f32)],
        compiler_params=pltpu.CompilerParams(
            dimension_semantics=("arbitrary", "arbitrary"),
            vmem_limit_bytes=_vmem_limit(blocks + scratch)),
        name="moba_attn",
    )(qkv, qkv, qkv, slope_rows)


def _conv_ln_kernel(prev_ref, cur_ref, w_ref, cb_ref, g_ref, b_ref, o_ref, ext_ref, y_ref, *, tm, lane_chunk, row_chunk):
    i = pl.program_id(0)
    c = cur_ref.shape[1]
    ext_ref[0:CONV_HALO, :] = jnp.where(i > 0, prev_ref[...], 0.0)
    ext_ref[CONV_HALO:CONV_HALO + tm, :] = cur_ref[...]
    first = CONV_HALO - (CONV_KERNEL - 1)
    for r0 in range(0, tm, row_chunk):
        for c0 in range(0, c, lane_chunk):
            acc = jnp.zeros((row_chunk, lane_chunk), f32)
            for k in range(CONV_KERNEL):
                acc = acc + (w_ref[k:k + 1, c0:c0 + lane_chunk]
                             * ext_ref[r0 + first + k:r0 + first + k + row_chunk, c0:c0 + lane_chunk])
            y_ref[r0:r0 + row_chunk, c0:c0 + lane_chunk] = acc + cb_ref[:, c0:c0 + lane_chunk]
    y = _layer_norm(y_ref[...], g_ref[...], b_ref[...])
    o_ref[...] = (y * jax.nn.sigmoid(y)).astype(o_ref.dtype)


def _conv_ln(hc, w, cb, g, b):
    s, c = hc.shape
    tm = _tile(s, 256)
    assert tm % CONV_HALO == 0 and CONV_HALO >= CONV_KERNEL - 1
    per = tm // CONV_HALO
    blocks = 2 * (CONV_HALO * c * 4 + tm * c * 4 + CONV_KERNEL * c * 4 + tm * c * 2)
    scratch = (tm + CONV_HALO) * c * 4 + tm * c * 4
    return pl.pallas_call(
        functools.partial(_conv_ln_kernel, tm=tm, lane_chunk=_tile(c, 512), row_chunk=_tile(tm, 32)),
        grid=(s // tm,),
        in_specs=[
            pl.BlockSpec((CONV_HALO, c), lambda i: (jnp.maximum(i * per - 1, 0), 0)),
            pl.BlockSpec((tm, c), lambda i: (i, 0)),
            pl.BlockSpec((CONV_KERNEL, c), lambda i: (0, 0)),
            pl.BlockSpec((1, c), lambda i: (0, 0)),
            pl.BlockSpec((1, c), lambda i: (0, 0)),
            pl.BlockSpec((1, c), lambda i: (0, 0)),
        ],
        out_specs=pl.BlockSpec((tm, c), lambda i: (i, 0)),
        out_shape=jax.ShapeDtypeStruct((s, c), bf16),
        scratch_shapes=[pltpu.VMEM((tm + CONV_HALO, c), f32), pltpu.VMEM((tm, c), f32)],
        compiler_params=pltpu.CompilerParams(
            dimension_semantics=("arbitrary",),
            vmem_limit_bytes=_vmem_limit(blocks + scratch)),
        name="conv_ln",
    )(hc, hc, w, cb, g, b)


def _merge_kernel(o_ref, hc_ref, ga_ref, gc_ref, wa_ref, wc_ref, m_ref):
    ya = jnp.dot(o_ref[...], wa_ref[...], preferred_element_type=f32)
    yc = jnp.dot(hc_ref[...], wc_ref[...], preferred_element_type=f32)
    m_ref[...] = (ga_ref[...].astype(f32) * ya + gc_ref[...].astype(f32) * yc).astype(m_ref.dtype)


def _merge(o, hc, gates, wa, wc):
    s, da = o.shape
    dc = hc.shape[1]
    d = wa.shape[1]
    tm = _tile(s, 1024)
    tn = _tile(d, 512)
    nj = d // tn
    blocks = 2 * (tm * da * 2 + tm * dc * 2 + 2 * tm * tn * 2 + da * tn * 2 + dc * tn * 2 + tm * tn * 2)
    return pl.pallas_call(
        _merge_kernel,
        grid=(s // tm, nj),
        in_specs=[
            pl.BlockSpec((tm, da), lambda i, j: (i, 0)),
            pl.BlockSpec((tm, dc), lambda i, j: (i, 0)),
            pl.BlockSpec((tm, tn), lambda i, j: (i, j)),
            pl.BlockSpec((tm, tn), lambda i, j: (i, nj + j)),
            pl.BlockSpec((da, tn), lambda i, j: (0, j)),
            pl.BlockSpec((dc, tn), lambda i, j: (0, j)),
        ],
        out_specs=pl.BlockSpec((tm, tn), lambda i, j: (i, j)),
        out_shape=jax.ShapeDtypeStruct((s, d), bf16),
        compiler_params=pltpu.CompilerParams(
            dimension_semantics=("arbitrary", "arbitrary"),
            vmem_limit_bytes=_vmem_limit(blocks)),
        name="merge",
    )(o, hc, gates, gates, wa, wc)


def _out_ln_kernel(m_ref, h_ref, w_ref, g_ref, b_ref, o_ref):
    mix = jnp.dot(m_ref[...], w_ref[...], preferred_element_type=f32)
    o_ref[...] = _layer_norm(DN_ALPHA * h_ref[...] + mix, g_ref[...], b_ref[...])


def _out_ln(m, h, w, g, b):
    s, d = h.shape
    tm = _tile(s, 512)
    blocks = 2 * (tm * d * 2 + tm * d * 4 + d * d * 2 + tm * d * 4)
    return pl.pallas_call(
        _out_ln_kernel,
        grid=(s // tm,),
        in_specs=[
            pl.BlockSpec((tm, d), lambda i: (i, 0)),
            pl.BlockSpec((tm, d), lambda i: (i, 0)),
            pl.BlockSpec((d, d), lambda i: (0, 0)),
            pl.BlockSpec((1, d), lambda i: (0, 0)),
            pl.BlockSpec((1, d), lambda i: (0, 0)),
        ],
        out_specs=pl.BlockSpec((tm, d), lambda i: (i, 0)),
        out_shape=jax.ShapeDtypeStruct((s, d), f32),
        compiler_params=pltpu.CompilerParams(
            dimension_semantics=("arbitrary",),
            vmem_limit_bytes=_vmem_limit(blocks)),
        name="out_ln",
    )(m, h, w, g, b)


def kernel(x, ln1_g, ln1_b, ffn1_wg, ffn1_wu, ffn1_wd, w_in, conv_dw, conv_db, conv_ln_g, conv_ln_b, w_conv_out, w_attn_out, w_out, ln2_g, ln2_b, ffn2_wg, ffn2_wu, ffn2_wd, ln3_g, ln3_b):
    batch, seq, d_model = x.shape
    depth = ffn1_wg.shape[0]
    assert depth == DEPTH
    attn_w = w_attn_out.shape[1]
    conv_w = w_conv_out.shape[1]
    n_heads = attn_w // HEAD_DIM
    c0, c1, c2 = 3 * attn_w, 3 * attn_w + conv_w, 3 * attn_w + 2 * conv_w
    row = lambda v: v.reshape(1, -1)

    outs = []
    for bi in range(batch):
        h = x[bi]
        for l in range(depth):
            h, hb = _ffn_ln(h, ffn1_wg[l].astype(bf16), ffn1_wu[l].astype(bf16), ffn1_wd[l].astype(bf16),
                            row(ln1_g[l]), row(ln1_b[l]))
            w = w_in[l]
            qkv = _proj(hb, w[:, :c0].astype(bf16), sigmoid=False, name="qkv_proj")
            glu = _glu_proj(hb, w[:, c0:c1].astype(bf16), w[:, c1:c2].astype(bf16))
            gates = _proj(hb, w[:, c2:].astype(bf16), sigmoid=True, name="gate_proj")
            o = _moba_attention(qkv, n_heads)
            hc = _conv_ln(glu, conv_dw[l].reshape(CONV_KERNEL, conv_w), row(conv_db[l]),
                          row(conv_ln_g[l]), row(conv_ln_b[l]))
            merged = _merge(o, hc, gates, w_attn_out[l].astype(bf16), w_conv_out[l].astype(bf16))
            h = _out_ln(merged, h, w_out[l].astype(bf16), row(ln2_g[l]), row(ln2_b[l]))
            h, _ = _ffn_ln(h, ffn2_wg[l].astype(bf16), ffn2_wu[l].astype(bf16), ffn2_wd[l].astype(bf16),
                           row(ln3_g[l]), row(ln3_b[l]))
        outs.append(h)
    return outs[0][None] if batch == 1 else jnp.stack(outs, axis=0)
```

```python
import functools

import numpy as np
import jax
import jax.numpy as jnp
from jax import lax
from jax.experimental import pallas as pl
from jax.experimental.pallas import tpu as pltpu

HEAD_DIM = 128
MOBA_BLOCK = 256
MOBA_TOPK = 3
MOBA_CHUNK = 8
EXP2_ZERO = 160.0
CONV_KERNEL = 31
LN_EPS = 1e-5
DEPTH = 1
DN_ALPHA = (2 * DEPTH) ** 0.25
ALIBI_MAX_LOG2 = 8.0
LOG2_E = 1.4426950408889634

V7X_VMEM_BYTES = 64 * 1024 * 1024
V7X_LANES = 128
V7X_SUBLANES = 8
CONV_HALO = 32
MASK_NEG = -1e30

f32 = jnp.float32
bf16 = jnp.bfloat16


def _vmem_limit(block_bytes):
    return int(min(V7X_VMEM_BYTES - 4 * 1024 * 1024, block_bytes + 16 * 1024 * 1024))


def _tile(dim, pref):
    if dim <= pref:
        return dim
    for t in range(pref, 0, -V7X_LANES):
        if dim % t == 0:
            return t
    raise ValueError((dim, pref))


def _layer_norm(y, g, b):
    mu = jnp.mean(y, axis=-1, keepdims=True)
    yc = y - mu
    var = jnp.mean(yc * yc, axis=-1, keepdims=True)
    return yc * lax.rsqrt(var + LN_EPS) * g + b


def _ffn_ln_kernel(x_ref, wg_ref, wu_ref, wd_ref, g_ref, b_ref, o_ref, ob_ref, xb_ref, acc_ref):
    f = pl.program_id(1)

    @pl.when(f == 0)
    def _():
        xb_ref[...] = x_ref[...].astype(bf16)
        acc_ref[...] = jnp.zeros_like(acc_ref)

    xb = xb_ref[...]
    gate = jnp.dot(xb, wg_ref[...], preferred_element_type=f32)
    up = jnp.dot(xb, wu_ref[...], preferred_element_type=f32)
    act = (gate * jax.nn.sigmoid(gate) * up).astype(bf16)
    acc_ref[...] += jnp.dot(act, wd_ref[...], preferred_element_type=f32)

    @pl.when(f == pl.num_programs(1) - 1)
    def _():
        y = DN_ALPHA * x_ref[...] + 0.5 * acc_ref[...]
        out = _layer_norm(y, g_ref[...], b_ref[...])
        o_ref[...] = out
        ob_ref[...] = out.astype(bf16)


def _ffn_ln(x, wg, wu, wd, g, b):
    s, d = x.shape
    ff = wg.shape[1]
    tm = _tile(s, 512)
    tf = _tile(ff, 512)
    blocks = 2 * (tm * d * 4 + 2 * d * tf * 2 + tf * d * 2 + tm * d * 4 + tm * d * 2)
    scratch = tm * d * 2 + tm * d * 4
    return pl.pallas_call(
        _ffn_ln_kernel,
        grid=(s // tm, ff // tf),
        in_specs=[
            pl.BlockSpec((tm, d), lambda i, j: (i, 0)),
            pl.BlockSpec((d, tf), lambda i, j: (0, j)),
            pl.BlockSpec((d, tf), lambda i, j: (0, j)),
            pl.BlockSpec((tf, d), lambda i, j: (j, 0)),
            pl.BlockSpec((1, d), lambda i, j: (0, 0)),
            pl.BlockSpec((1, d), lambda i, j: (0, 0)),
        ],
        out_specs=[
            pl.BlockSpec((tm, d), lambda i, j: (i, 0)),
            pl.BlockSpec((tm, d), lambda i, j: (i, 0)),
        ],
        out_shape=[jax.ShapeDtypeStruct((s, d), f32), jax.ShapeDtypeStruct((s, d), bf16)],
        scratch_shapes=[pltpu.VMEM((tm, d), bf16), pltpu.VMEM((tm, d), f32)],
        compiler_params=pltpu.CompilerParams(
            dimension_semantics=("arbitrary", "arbitrary"),
            vmem_limit_bytes=_vmem_limit(blocks + scratch)),
        name="ffn_ln",
    )(x, wg, wu, wd, g, b)


def _proj_kernel(x_ref, w_ref, o_ref, *, sigmoid):
    y = jnp.dot(x_ref[...], w_ref[...], preferred_element_type=f32)
    if sigmoid:
        y = jax.nn.sigmoid(y)
    o_ref[...] = y.astype(o_ref.dtype)


def _proj(x, w, *, sigmoid, name):
    s, d = x.shape
    n = w.shape[1]
    tm = _tile(s, 1024)
    tn = _tile(n, 1024)
    blocks = 2 * (tm * d * 2 + d * tn * 2 + tm * tn * 2)
    return pl.pallas_call(
        functools.partial(_proj_kernel, sigmoid=sigmoid),
        grid=(s // tm, n // tn),
        in_specs=[
            pl.BlockSpec((tm, d), lambda i, j: (i, 0)),
            pl.BlockSpec((d, tn), lambda i, j: (0, j)),
        ],
        out_specs=pl.BlockSpec((tm, tn), lambda i, j: (i, j)),
        out_shape=jax.ShapeDtypeStruct((s, n), bf16),
        compiler_params=pltpu.CompilerParams(
            dimension_semantics=("arbitrary", "arbitrary"),
            vmem_limit_bytes=_vmem_limit(blocks)),
        name=name,
    )(x, w)


def _glu_proj_kernel(x_ref, wa_ref, wg_ref, o_ref):
    x = x_ref[...]
    a = jnp.dot(x, wa_ref[...], preferred_element_type=f32)
    g = jnp.dot(x, wg_ref[...], preferred_element_type=f32)
    o_ref[...] = a * jax.nn.sigmoid(g)


def _glu_proj(x, wa, wg):
    s, d = x.shape
    n = wa.shape[1]
    tm = _tile(s, 1024)
    tn = _tile(n, 512)
    blocks = 2 * (tm * d * 2 + 2 * d * tn * 2 + tm * tn * 4)
    return pl.pallas_call(
        _glu_proj_kernel,
        grid=(s // tm, n // tn),
        in_specs=[
            pl.BlockSpec((tm, d), lambda i, j: (i, 0)),
            pl.BlockSpec((d, tn), lambda i, j: (0, j)),
            pl.BlockSpec((d, tn), lambda i, j: (0, j)),
        ],
        out_specs=pl.BlockSpec((tm, tn), lambda i, j: (i, j)),
        out_shape=jax.ShapeDtypeStruct((s, n), f32),
        compiler_params=pltpu.CompilerParams(
            dimension_semantics=("arbitrary", "arbitrary"),
            vmem_limit_bytes=_vmem_limit(blocks)),
        name="glu_proj",
    )(x, wa, wg)


_NT = (((1,), (1,)), ((), ()))


def _moba_kernel(q_ref, qn_ref, k_ref, v_ref, slope_ref, o_ref, kaug_ref, kmean_ref, kmax_ref, qaug_ref, qnext_ref, s_ref, *, nb, chunk, scale):
    blk = MOBA_BLOCK
    span = chunk * blk
    pad = chunk - 1
    qb = pl.program_id(1)

    def make_qaug(q, n_past):
        gate = lax.dot_general(kmean_ref[...].astype(bf16), q, _NT, preferred_element_type=f32)
        row = lax.broadcasted_iota(jnp.int32, (V7X_LANES, blk), 0)
        neg_inf = f32(-jnp.inf)
        gate = jnp.where(row < n_past, gate, neg_inf)
        keep = jnp.zeros((V7X_LANES, blk), jnp.bool_)
        for r in range(min(MOBA_TOPK, nb)):
            top = jnp.max(gate, axis=0, keepdims=True)
            idx = jnp.min(jnp.where(gate == top, row, V7X_LANES), axis=0, keepdims=True)
            hit = row == idx
            keep = jnp.logical_or(keep, jnp.logical_and(hit, r < n_past))
            gate = jnp.where(hit, neg_inf, gate)
        maskbias = jnp.where(keep, 0.0, MASK_NEG).T.astype(bf16)
        return jnp.concatenate([(q.astype(f32) * scale).astype(bf16), maskbias], axis=1)

    @pl.when(qb == 0)
    def _build_head():
        lane = lax.broadcasted_iota(jnp.int32, (blk, V7X_LANES), 1)
        kmean_ref[...] = jnp.zeros_like(kmean_ref)
        dead = jnp.where(lane == V7X_LANES - 1, 1.0, 0.0).astype(bf16)
        for j in range(pad):
            kaug_ref[j * blk:(j + 1) * blk, :] = jnp.concatenate([jnp.zeros((blk, HEAD_DIM), bf16), dead], axis=1)

        def body(j, kmax):
            kj = k_ref[pl.ds(pl.multiple_of(j * blk, blk), blk), :]
            onehot = jnp.where(lane == j, 1.0, 0.0).astype(bf16)
            kaug_ref[pl.ds(pl.multiple_of((j + pad) * blk, blk), blk), :] = jnp.concatenate([kj, onehot], axis=1)
            kf = kj.astype(f32)
            kmean_ref[pl.ds(j, 1), :] = jnp.mean(kf, axis=0, keepdims=True)
            k2 = jnp.max(jnp.sum(kf * kf, axis=1, keepdims=True), axis=0, keepdims=True)
            return jnp.maximum(kmax, k2)

        kmax_ref[...] = jnp.broadcast_to(lax.fori_loop(0, nb, body, jnp.zeros((1, 1), f32)), kmax_ref.shape)
        qnext_ref[...] = make_qaug(q_ref[...], 0)

    qaug_ref[...] = qnext_ref[...]
    qnext_ref[...] = make_qaug(qn_ref[...], qb + 1)
    qs = (q_ref[...].astype(f32) * scale).astype(bf16)

    slope = slope_ref[0]
    cpos = lax.broadcasted_iota(jnp.int32, (1, blk), 1).astype(f32)
    half = blk // 2

    qf = qs.astype(f32)
    q2 = jnp.max(jnp.sum(qf * qf, axis=1, keepdims=True), axis=0, keepdims=True)
    smax = jnp.sqrt(q2 * kmax_ref[0:1, 0:1])
    live = jnp.floor((2.0 * smax + EXP2_ZERO) / (slope[:, 0:1] * span)) + 1.0
    live = jnp.max(jnp.minimum(live, float(nb)).astype(jnp.int32))
    n_chunks = jnp.minimum(lax.div(qb + (chunk - 1), chunk), live)

    def lane_fold(x, op):
        return op(x[:, :half], x[:, half:])

    d0 = pl.multiple_of(qb * blk, blk)
    s = lax.dot_general(qs, k_ref[pl.ds(d0, blk), :], _NT, preferred_element_type=f32)
    s = s + slope * cpos
    rr = lax.broadcasted_iota(jnp.int32, (blk, blk), 0)
    cc = lax.broadcasted_iota(jnp.int32, (blk, blk), 1)
    s = jnp.where(cc <= rr, s, MASK_NEG)
    m0 = jnp.max(s, axis=1, keepdims=True)
    p = jnp.exp2(s - m0)
    l0 = lane_fold(p, jnp.add)
    acc0 = jnp.dot(p.astype(bf16), v_ref[pl.ds(d0, blk), :], preferred_element_type=f32)

    def first_block(c):
        return qb - chunk * (c + 1)

    def score_pass(c):
        b0 = jnp.maximum(first_block(c) + pad, 0)
        mx = None
        for t in range(chunk):
            r0 = pl.multiple_of((b0 + t) * blk, blk)
            st = lax.dot_general(qaug_ref[...], kaug_ref[pl.ds(r0, blk), :], _NT,
                                 preferred_element_type=f32)
            dist = (chunk * (c + 1) - t) * blk
            st = st + slope * (cpos - jnp.asarray(dist, f32))
            s_ref[:, t * blk:(t + 1) * blk] = st
            e = lane_fold(st, jnp.maximum)
            mx = e if mx is None else jnp.maximum(mx, e)
        return jnp.max(mx, axis=1, keepdims=True)

    def value_pass(c, m_new):
        ls = jnp.zeros((blk, half), f32)
        pv = jnp.zeros((blk, HEAD_DIM), f32)
        for t in range(chunk):
            r0 = pl.multiple_of(jnp.maximum(first_block(c) + t, 0) * blk, blk)
            pt = jnp.exp2(s_ref[:, t * blk:(t + 1) * blk] - m_new)
            ls = ls + lane_fold(pt, jnp.add)
            pv = pv + jnp.dot(pt.astype(bf16), v_ref[pl.ds(r0, blk), :], preferred_element_type=f32)
        return ls, pv

    def body(c, carry):
        m, l, acc, mx_c = carry
        m_new = jnp.maximum(m, mx_c)
        alpha = jnp.exp2(m - m_new)
        ls, pv = value_pass(c, m_new)
        mx_next = score_pass(c + 1)
        return m_new, alpha * l + ls, alpha * acc + pv, mx_next

    mx0 = score_pass(0)

    m, l, acc, mx_c = lax.fori_loop(0, n_chunks - 1, body, (m0, l0, acc0, mx0))
    some = n_chunks > 0
    m_new = jnp.maximum(m, jnp.where(some, mx_c, m))
    alpha = jnp.exp2(m - m_new)
    ls, pv = value_pass(jnp.maximum(n_chunks - 1, 0), jnp.where(some, m_new, -MASK_NEG))
    l = alpha * l + ls
    acc = alpha * acc + pv
    o_ref[...] = (acc / jnp.sum(l, axis=1, keepdims=True)).astype(o_ref.dtype)


def _moba_attention(qkv, n_heads):
    s = qkv.shape[0]
    blk = MOBA_BLOCK
    assert s % blk == 0
    nb = s // blk
    assert nb < V7X_LANES
    chunk = min(MOBA_CHUNK, nb)
    span = chunk * blk
    kaug_rows = (nb + chunk - 1) * blk
    slopes = np.array([2.0 ** (-ALIBI_MAX_LOG2 * (h + 1) / n_heads) for h in range(n_heads)], np.float32)
    slopes = (slopes.astype(np.float64) * LOG2_E).astype(np.float32)
    slope_rows = jnp.asarray(np.broadcast_to(slopes[:, None, None], (n_heads, 1, blk)).copy())
    blocks = 2 * (2 * blk * HEAD_DIM * 2 + 2 * s * HEAD_DIM * 2 + blk * 4 + blk * HEAD_DIM * 2)
    scratch = (kaug_rows * 2 * HEAD_DIM * 2 + V7X_LANES * HEAD_DIM * 4 + V7X_SUBLANES * V7X_LANES * 4
               + 2 * blk * 2 * HEAD_DIM * 2 + blk * span * 4)
    return pl.pallas_call(
        functools.partial(_moba_kernel, nb=nb, chunk=chunk, scale=HEAD_DIM ** -0.5 * LOG2_E),
        grid=(n_heads, nb),
        in_specs=[
            pl.BlockSpec((blk, HEAD_DIM), lambda h, i: (i, h)),
            pl.BlockSpec((blk, HEAD_DIM), lambda h, i: (jnp.minimum(i + 1, nb - 1), h)),
            pl.BlockSpec((s, HEAD_DIM), lambda h, i: (0, n_heads + h)),
            pl.BlockSpec((s, HEAD_DIM), lambda h, i: (0, 2 * n_heads + h)),
            pl.BlockSpec((1, 1, blk), lambda h, i: (h, 0, 0)),
        ],
        out_specs=pl.BlockSpec((blk, HEAD_DIM), lambda h, i: (i, h)),
        out_shape=jax.ShapeDtypeStruct((s, n_heads * HEAD_DIM), bf16),
        scratch_shapes=[pltpu.VMEM((kaug_rows, 2 * HEAD_DIM), bf16), pltpu.VMEM((V7X_LANES, HEAD_DIM), f32),
                        pltpu.VMEM((V7X_SUBLANES, V7X_LANES), f32),
                        pltpu.VMEM((blk, 2 * HEAD_DIM), bf16), pltpu.VMEM((blk, 2 * HEAD_DIM), bf16),
                        pltpu.VMEM((blk, span), f32)],
        compiler_params=pltpu.CompilerParams(
            dimension_semantics=("arbitrary", "arbitrary"),
            vmem_limit_bytes=_vmem_limit(blocks + scratch)),
        name="moba_attn",
    )(qkv, qkv, qkv, qkv, slope_rows)


def _conv_ln_kernel(prev_ref, cur_ref, w_ref, cb_ref, g_ref, b_ref, o_ref, ext_ref, y_ref, *, tm, lane_chunk, row_chunk):
    i = pl.program_id(0)
    c = cur_ref.shape[1]
    ext_ref[0:CONV_HALO, :] = jnp.where(i > 0, prev_ref[...], 0.0)
    ext_ref[CONV_HALO:CONV_HALO + tm, :] = cur_ref[...]
    first = CONV_HALO - (CONV_KERNEL - 1)
    for r0 in range(0, tm, row_chunk):
        for c0 in range(0, c, lane_chunk):
            acc = jnp.zeros((row_chunk, lane_chunk), f32)
            for k in range(CONV_KERNEL):
                acc = acc + (w_ref[k:k + 1, c0:c0 + lane_chunk]
                             * ext_ref[r0 + first + k:r0 + first + k + row_chunk, c0:c0 + lane_chunk])
            y_ref[r0:r0 + row_chunk, c0:c0 + lane_chunk] = acc + cb_ref[:, c0:c0 + lane_chunk]
    y = _layer_norm(y_ref[...], g_ref[...], b_ref[...])
    o_ref[...] = (y * jax.nn.sigmoid(y)).astype(o_ref.dtype)


def _conv_ln(hc, w, cb, g, b):
    s, c = hc.shape
    tm = _tile(s, 256)
    assert tm % CONV_HALO == 0 and CONV_HALO >= CONV_KERNEL - 1
    per = tm // CONV_HALO
    blocks = 2 * (CONV_HALO * c * 4 + tm * c * 4 + CONV_KERNEL * c * 4 + tm * c * 2)
    scratch = (tm + CONV_HALO) * c * 4 + tm * c * 4
    return pl.pallas_call(
        functools.partial(_conv_ln_kernel, tm=tm, lane_chunk=_tile(c, 512), row_chunk=_tile(tm, 32)),
        grid=(s // tm,),
        in_specs=[
            pl.BlockSpec((CONV_HALO, c), lambda i: (jnp.maximum(i * per - 1, 0), 0)),
            pl.BlockSpec((tm, c), lambda i: (i, 0)),
            pl.BlockSpec((CONV_KERNEL, c), lambda i: (0, 0)),
            pl.BlockSpec((1, c), lambda i: (0, 0)),
            pl.BlockSpec((1, c), lambda i: (0, 0)),
            pl.BlockSpec((1, c), lambda i: (0, 0)),
        ],
        out_specs=pl.BlockSpec((tm, c), lambda i: (i, 0)),
        out_shape=jax.ShapeDtypeStruct((s, c), bf16),
        scratch_shapes=[pltpu.VMEM((tm + CONV_HALO, c), f32), pltpu.VMEM((tm, c), f32)],
        compiler_params=pltpu.CompilerParams(
            dimension_semantics=("arbitrary",),
            vmem_limit_bytes=_vmem_limit(blocks + scratch)),
        name="conv_ln",
    )(hc, hc, w, cb, g, b)


def _merge_kernel(o_ref, hc_ref, ga_ref, gc_ref, wa_ref, wc_ref, m_ref):
    ya = jnp.dot(o_ref[...], wa_ref[...], preferred_element_type=f32)
    yc = jnp.dot(hc_ref[...], wc_ref[...], preferred_element_type=f32)
    m_ref[...] = (ga_ref[...].astype(f32) * ya + gc_ref[...].astype(f32) * yc).astype(m_ref.dtype)


def _merge(o, hc, gates, wa, wc):
    s, da = o.shape
    dc = hc.shape[1]
    d = wa.shape[1]
    tm = _tile(s, 1024)
    tn = _tile(d, 512)
    nj = d // tn
    blocks = 2 * (tm * da * 2 + tm * dc * 2 + 2 * tm * tn * 2 + da * tn * 2 + dc * tn * 2 + tm * tn * 2)
    return pl.pallas_call(
        _merge_kernel,
        grid=(s // tm, nj),
        in_specs=[
            pl.BlockSpec((tm, da), lambda i, j: (i, 0)),
            pl.BlockSpec((tm, dc), lambda i, j: (i, 0)),
            pl.BlockSpec((tm, tn), lambda i, j: (i, j)),
            pl.BlockSpec((tm, tn), lambda i, j: (i, nj + j)),
            pl.BlockSpec((da, tn), lambda i, j: (0, j)),
            pl.BlockSpec((dc, tn), lambda i, j: (0, j)),
        ],
        out_specs=pl.BlockSpec((tm, tn), lambda i, j: (i, j)),
        out_shape=jax.ShapeDtypeStruct((s, d), bf16),
        compiler_params=pltpu.CompilerParams(
            dimension_semantics=("arbitrary", "arbitrary"),
            vmem_limit_bytes=_vmem_limit(blocks)),
        name="merge",
    )(o, hc, gates, gates, wa, wc)


def _out_ln_kernel(m_ref, h_ref, w_ref, g_ref, b_ref, o_ref):
    mix = jnp.dot(m_ref[...], w_ref[...], preferred_element_type=f32)
    o_ref[...] = _layer_norm(DN_ALPHA * h_ref[...] + mix, g_ref[...], b_ref[...])


def _out_ln(m, h, w, g, b):
    s, d = h.shape
    tm = _tile(s, 512)
    blocks = 2 * (tm * d * 2 + tm * d * 4 + d * d * 2 + tm * d * 4)
    return pl.pallas_call(
        _out_ln_kernel,
        grid=(s // tm,),
        in_specs=[
            pl.BlockSpec((tm, d), lambda i: (i, 0)),
            pl.BlockSpec((tm, d), lambda i: (i, 0)),
            pl.BlockSpec((d, d), lambda i: (0, 0)),
            pl.BlockSpec((1, d), lambda i: (0, 0)),
            pl.BlockSpec((1, d), lambda i: (0, 0)),
        ],
        out_specs=pl.BlockSpec((tm, d), lambda i: (i, 0)),
        out_shape=jax.ShapeDtypeStruct((s, d), f32),
        compiler_params=pltpu.CompilerParams(
            dimension_semantics=("arbitrary",),
            vmem_limit_bytes=_vmem_limit(blocks)),
        name="out_ln",
    )(m, h, w, g, b)


def kernel(x, ln1_g, ln1_b, ffn1_wg, ffn1_wu, ffn1_wd, w_in, conv_dw, conv_db, conv_ln_g, conv_ln_b, w_conv_out, w_attn_out, w_out, ln2_g, ln2_b, ffn2_wg, ffn2_wu, ffn2_wd, ln3_g, ln3_b):
    batch, seq, d_model = x.shape
    depth = ffn1_wg.shape[0]
    assert depth == DEPTH
    attn_w = w_attn_out.shape[1]
    conv_w = w_conv_out.shape[1]
    n_heads = attn_w // HEAD_DIM
    c0, c1, c2 = 3 * attn_w, 3 * attn_w + conv_w, 3 * attn_w + 2 * conv_w
    row = lambda v: v.reshape(1, -1)

    outs = []
    for bi in range(batch):
        h = x[bi]
        for l in range(depth):
            h, hb = _ffn_ln(h, ffn1_wg[l].astype(bf16), ffn1_wu[l].astype(bf16), ffn1_wd[l].astype(bf16),
                            row(ln1_g[l]), row(ln1_b[l]))
            w = w_in[l]
            qkv = _proj(hb, w[:, :c0].astype(bf16), sigmoid=False, name="qkv_proj")
            glu = _glu_proj(hb, w[:, c0:c1].astype(bf16), w[:, c1:c2].astype(bf16))
            gates = _proj(hb, w[:, c2:].astype(bf16), sigmoid=True, name="gate_proj")
            o = _moba_attention(qkv, n_heads)
            hc = _conv_ln(glu, conv_dw[l].reshape(CONV_KERNEL, conv_w), row(conv_db[l]),
                          row(conv_ln_g[l]), row(conv_ln_b[l]))
            merged = _merge(o, hc, gates, w_attn_out[l].astype(bf16), w_conv_out[l].astype(bf16))
            h = _out_ln(merged, h, w_out[l].astype(bf16), row(ln2_g[l]), row(ln2_b[l]))
            h, _ = _ffn_ln(h, ffn2_wg[l].astype(bf16), ffn2_wu[l].astype(bf16), ffn2_wd[l].astype(bf16),
                           row(ln3_g[l]), row(ln3_b[l]))
        outs.append(h)
    return outs[0][None] if batch == 1 else jnp.stack(outs, axis=0)
```

```python
import functools

import numpy as np
import jax
import jax.numpy as jnp
from jax import lax
from jax.experimental import pallas as pl
from jax.experimental.pallas import tpu as pltpu

HEAD_DIM = 128
MOBA_BLOCK = 256
MOBA_TOPK = 3
MOBA_CHUNK = 8
EXP2_ZERO = 160.0
CONV_KERNEL = 31
LN_EPS = 1e-5
DEPTH = 1
DN_ALPHA = (2 * DEPTH) ** 0.25
ALIBI_MAX_LOG2 = 8.0
LOG2_E = 1.4426950408889634

V7X_VMEM_BYTES = 64 * 1024 * 1024
V7X_LANES = 128
V7X_SUBLANES = 8
CONV_HALO = 32
MASK_NEG = -1e30

f32 = jnp.float32
bf16 = jnp.bfloat16


def _vmem_limit(block_bytes):
    return int(min(V7X_VMEM_BYTES - 4 * 1024 * 1024, block_bytes + 16 * 1024 * 1024))


def _tile(dim, pref):
    if dim <= pref:
        return dim
    for t in range(pref, 0, -V7X_LANES):
        if dim % t == 0:
            return t
    raise ValueError((dim, pref))


def _col_blocks(w, tn):
    k, n = w.shape
    return w.reshape(k, n // tn, tn).transpose(1, 0, 2)


def _col_block_spec(k, tn):
    return pl.BlockSpec((None, k, tn), lambda i, j: (j, 0, 0))


def _layer_norm(y, g, b):
    mu = jnp.mean(y, axis=-1, keepdims=True)
    yc = y - mu
    var = jnp.mean(yc * yc, axis=-1, keepdims=True)
    return yc * lax.rsqrt(var + LN_EPS) * g + b


def _ffn_ln_kernel(x_ref, wg_ref, wu_ref, wd_ref, g_ref, b_ref, o_ref, ob_ref, xb_ref, acc_ref):
    f = pl.program_id(1)

    @pl.when(f == 0)
    def _():
        xb_ref[...] = x_ref[...].astype(bf16)
        acc_ref[...] = jnp.zeros_like(acc_ref)

    xb = xb_ref[...]
    gate = jnp.dot(xb, wg_ref[...], preferred_element_type=f32)
    up = jnp.dot(xb, wu_ref[...], preferred_element_type=f32)
    act = (gate * jax.nn.sigmoid(gate) * up).astype(bf16)
    acc_ref[...] += jnp.dot(act, wd_ref[...], preferred_element_type=f32)

    @pl.when(f == pl.num_programs(1) - 1)
    def _():
        y = DN_ALPHA * x_ref[...] + 0.5 * acc_ref[...]
        out = _layer_norm(y, g_ref[...], b_ref[...])
        o_ref[...] = out
        ob_ref[...] = out.astype(bf16)


def _ffn_ln(x, wg, wu, wd, g, b):
    s, d = x.shape
    ff = wg.shape[1]
    tm = _tile(s, 512)
    tf = _tile(ff, 512)
    blocks = 2 * (tm * d * 4 + 2 * d * tf * 2 + tf * d * 2 + tm * d * 4 + tm * d * 2)
    scratch = tm * d * 2 + tm * d * 4
    return pl.pallas_call(
        _ffn_ln_kernel,
        grid=(s // tm, ff // tf),
        in_specs=[
            pl.BlockSpec((tm, d), lambda i, j: (i, 0)),
            _col_block_spec(d, tf),
            _col_block_spec(d, tf),
            pl.BlockSpec((tf, d), lambda i, j: (j, 0)),
            pl.BlockSpec((1, d), lambda i, j: (0, 0)),
            pl.BlockSpec((1, d), lambda i, j: (0, 0)),
        ],
        out_specs=[
            pl.BlockSpec((tm, d), lambda i, j: (i, 0)),
            pl.BlockSpec((tm, d), lambda i, j: (i, 0)),
        ],
        out_shape=[jax.ShapeDtypeStruct((s, d), f32), jax.ShapeDtypeStruct((s, d), bf16)],
        scratch_shapes=[pltpu.VMEM((tm, d), bf16), pltpu.VMEM((tm, d), f32)],
        compiler_params=pltpu.CompilerParams(
            dimension_semantics=("arbitrary", "arbitrary"),
            vmem_limit_bytes=_vmem_limit(blocks + scratch)),
        name="ffn_ln",
    )(x, _col_blocks(wg, tf), _col_blocks(wu, tf), wd, g, b)


def _proj_kernel(x_ref, w_ref, o_ref, *, sigmoid):
    y = jnp.dot(x_ref[...], w_ref[...], preferred_element_type=f32)
    if sigmoid:
        y = jax.nn.sigmoid(y)
    o_ref[...] = y.astype(o_ref.dtype)


def _proj(x, w, *, sigmoid, name):
    s, d = x.shape
    n = w.shape[1]
    tm = _tile(s, 1024)
    tn = _tile(n, 1024)
    blocks = 2 * (tm * d * 2 + d * tn * 2 + tm * tn * 2)
    return pl.pallas_call(
        functools.partial(_proj_kernel, sigmoid=sigmoid),
        grid=(s // tm, n // tn),
        in_specs=[
            pl.BlockSpec((tm, d), lambda i, j: (i, 0)),
            _col_block_spec(d, tn),
        ],
        out_specs=pl.BlockSpec((tm, tn), lambda i, j: (i, j)),
        out_shape=jax.ShapeDtypeStruct((s, n), bf16),
        compiler_params=pltpu.CompilerParams(
            dimension_semantics=("arbitrary", "arbitrary"),
            vmem_limit_bytes=_vmem_limit(blocks)),
        name=name,
    )(x, _col_blocks(w, tn))


def _glu_proj_kernel(x_ref, wa_ref, wg_ref, o_ref):
    x = x_ref[...]
    a = jnp.dot(x, wa_ref[...], preferred_element_type=f32)
    g = jnp.dot(x, wg_ref[...], preferred_element_type=f32)
    o_ref[...] = a * jax.nn.sigmoid(g)


def _glu_proj(x, wa, wg):
    s, d = x.shape
    n = wa.shape[1]
    tm = _tile(s, 1024)
    tn = _tile(n, 512)
    blocks = 2 * (tm * d * 2 + 2 * d * tn * 2 + tm * tn * 4)
    return pl.pallas_call(
        _glu_proj_kernel,
        grid=(s // tm, n // tn),
        in_specs=[
            pl.BlockSpec((tm, d), lambda i, j: (i, 0)),
            _col_block_spec(d, tn),
            _col_block_spec(d, tn),
        ],
        out_specs=pl.BlockSpec((tm, tn), lambda i, j: (i, j)),
        out_shape=jax.ShapeDtypeStruct((s, n), f32),
        compiler_params=pltpu.CompilerParams(
            dimension_semantics=("arbitrary", "arbitrary"),
            vmem_limit_bytes=_vmem_limit(blocks)),
        name="glu_proj",
    )(x, _col_blocks(wa, tn), _col_blocks(wg, tn))


_NT = (((1,), (1,)), ((), ()))


def _moba_kernel(q_ref, qn_ref, k_ref, v_ref, slope_ref, o_ref, kaug_ref, kmean_ref, kmax_ref, qaug_ref, qnext_ref, s_ref, *, nb, chunk, scale):
    blk = MOBA_BLOCK
    span = chunk * blk
    pad = chunk - 1
    qb = pl.program_id(1)

    def make_qaug(q, n_past):
        gate = lax.dot_general(kmean_ref[...].astype(bf16), q, _NT, preferred_element_type=f32)
        row = lax.broadcasted_iota(jnp.int32, (V7X_LANES, blk), 0)
        neg_inf = f32(-jnp.inf)
        gate = jnp.where(row < n_past, gate, neg_inf)
        keep = jnp.zeros((V7X_LANES, blk), jnp.bool_)
        for r in range(min(MOBA_TOPK, nb)):
            top = jnp.max(gate, axis=0, keepdims=True)
            idx = jnp.min(jnp.where(gate == top, row, V7X_LANES), axis=0, keepdims=True)
            hit = row == idx
            keep = jnp.logical_or(keep, jnp.logical_and(hit, r < n_past))
            gate = jnp.where(hit, neg_inf, gate)
        maskbias = jnp.where(keep, 0.0, MASK_NEG).T.astype(bf16)
        return jnp.concatenate([(q.astype(f32) * scale).astype(bf16), maskbias], axis=1)

    @pl.when(qb == 0)
    def _build_head():
        lane = lax.broadcasted_iota(jnp.int32, (blk, V7X_LANES), 1)
        kmean_ref[...] = jnp.zeros_like(kmean_ref)
        dead = jnp.where(lane == V7X_LANES - 1, 1.0, 0.0).astype(bf16)
        for j in range(pad):
            kaug_ref[j * blk:(j + 1) * blk, :] = jnp.concatenate([jnp.zeros((blk, HEAD_DIM), bf16), dead], axis=1)

        def body(j, kmax):
            kj = k_ref[pl.ds(pl.multiple_of(j * blk, blk), blk), :]
            onehot = jnp.where(lane == j, 1.0, 0.0).astype(bf16)
            kaug_ref[pl.ds(pl.multiple_of((j + pad) * blk, blk), blk), :] = jnp.concatenate([kj, onehot], axis=1)
            kf = kj.astype(f32)
            kmean_ref[pl.ds(j, 1), :] = jnp.mean(kf, axis=0, keepdims=True)
            k2 = jnp.max(jnp.sum(kf * kf, axis=1, keepdims=True), axis=0, keepdims=True)
            return jnp.maximum(kmax, k2)

        kmax_ref[...] = jnp.broadcast_to(lax.fori_loop(0, nb, body, jnp.zeros((1, 1), f32)), kmax_ref.shape)
        qnext_ref[...] = make_qaug(q_ref[...], 0)

    qaug_ref[...] = qnext_ref[...]
    qnext_ref[...] = make_qaug(qn_ref[...], qb + 1)
    qs = (q_ref[...].astype(f32) * scale).astype(bf16)

    slope = slope_ref[0]
    cpos = lax.broadcasted_iota(jnp.int32, (1, blk), 1).astype(f32)
    half = blk // 2

    qf = qs.astype(f32)
    q2 = jnp.max(jnp.sum(qf * qf, axis=1, keepdims=True), axis=0, keepdims=True)
    smax = jnp.sqrt(q2 * kmax_ref[0:1, 0:1])
    live = jnp.floor((2.0 * smax + EXP2_ZERO) / (slope[:, 0:1] * span)) + 1.0
    live = jnp.max(jnp.minimum(live, float(nb)).astype(jnp.int32))
    n_chunks = jnp.minimum(lax.div(qb + (chunk - 1), chunk), live)

    def lane_fold(x, op):
        return op(x[:, :half], x[:, half:])

    d0 = pl.multiple_of(qb * blk, blk)
    s = lax.dot_general(qs, k_ref[pl.ds(d0, blk), :], _NT, preferred_element_type=f32)
    s = s + slope * cpos
    rr = lax.broadcasted_iota(jnp.int32, (blk, blk), 0)
    cc = lax.broadcasted_iota(jnp.int32, (blk, blk), 1)
    s = jnp.where(cc <= rr, s, MASK_NEG)
    m0 = jnp.max(s, axis=1, keepdims=True)
    p = jnp.exp2(s - m0)
    l0 = lane_fold(p, jnp.add)
    acc0 = jnp.dot(p.astype(bf16), v_ref[pl.ds(d0, blk), :], preferred_element_type=f32)

    def first_block(c):
        return qb - chunk * (c + 1)

    def score_pass(c):
        b0 = jnp.maximum(first_block(c) + pad, 0)
        mx = None
        for t in range(chunk):
            r0 = pl.multiple_of((b0 + t) * blk, blk)
            st = lax.dot_general(qaug_ref[...], kaug_ref[pl.ds(r0, blk), :], _NT,
                                 preferred_element_type=f32)
            dist = (chunk * (c + 1) - t) * blk
            st = st + slope * (cpos - jnp.asarray(dist, f32))
            s_ref[:, t * blk:(t + 1) * blk] = st
            e = lane_fold(st, jnp.maximum)
            mx = e if mx is None else jnp.maximum(mx, e)
        return jnp.max(mx, axis=1, keepdims=True)

    def value_pass(c, m_new):
        ls = jnp.zeros((blk, half), f32)
        pv = jnp.zeros((blk, HEAD_DIM), f32)
        for t in range(chunk):
            r0 = pl.multiple_of(jnp.maximum(first_block(c) + t, 0) * blk, blk)
            pt = jnp.exp2(s_ref[:, t * blk:(t + 1) * blk] - m_new)
            ls = ls + lane_fold(pt, jnp.add)
            pv = pv + jnp.dot(pt.astype(bf16), v_ref[pl.ds(r0, blk), :], preferred_element_type=f32)
        return ls, pv

    def body(c, carry):
        m, l, acc, mx_c = carry
        m_new = jnp.maximum(m, mx_c)
        alpha = jnp.exp2(m - m_new)
        ls, pv = value_pass(c, m_new)
        mx_next = score_pass(c + 1)
        return m_new, alpha * l + ls, alpha * acc + pv, mx_next

    mx0 = score_pass(0)

    m, l, acc, mx_c = lax.fori_loop(0, n_chunks - 1, body, (m0, l0, acc0, mx0))
    some = n_chunks > 0
    m_new = jnp.maximum(m, jnp.where(some, mx_c, m))
    alpha = jnp.exp2(m - m_new)
    ls, pv = value_pass(jnp.maximum(n_chunks - 1, 0), jnp.where(some, m_new, -MASK_NEG))
    l = alpha * l + ls
    acc = alpha * acc + pv
    o_ref[...] = (acc / jnp.sum(l, axis=1, keepdims=True)).astype(o_ref.dtype)


def _moba_attention(qkv, n_heads):
    s = qkv.shape[0]
    blk = MOBA_BLOCK
    assert s % blk == 0
    nb = s // blk
    assert nb < V7X_LANES
    chunk = min(MOBA_CHUNK, nb)
    span = chunk * blk
    kaug_rows = (nb + chunk - 1) * blk
    slopes = np.array([2.0 ** (-ALIBI_MAX_LOG2 * (h + 1) / n_heads) for h in range(n_heads)], np.float32)
    slopes = (slopes.astype(np.float64) * LOG2_E).astype(np.float32)
    slope_rows = jnp.asarray(np.broadcast_to(slopes[:, None, None], (n_heads, 1, blk)).copy())
    blocks = 2 * (2 * blk * HEAD_DIM * 2 + 2 * s * HEAD_DIM * 2 + blk * 4 + blk * HEAD_DIM * 2)
    scratch = (kaug_rows * 2 * HEAD_DIM * 2 + V7X_LANES * HEAD_DIM * 4 + V7X_SUBLANES * V7X_LANES * 4
               + 2 * blk * 2 * HEAD_DIM * 2 + blk * span * 4)
    return pl.pallas_call(
        functools.partial(_moba_kernel, nb=nb, chunk=chunk, scale=HEAD_DIM ** -0.5 * LOG2_E),
        grid=(n_heads, nb),
        in_specs=[
            pl.BlockSpec((blk, HEAD_DIM), lambda h, i: (i, h)),
            pl.BlockSpec((blk, HEAD_DIM), lambda h, i: (jnp.minimum(i + 1, nb - 1), h)),
            pl.BlockSpec((s, HEAD_DIM), lambda h, i: (0, n_heads + h)),
            pl.BlockSpec((s, HEAD_DIM), lambda h, i: (0, 2 * n_heads + h)),
            pl.BlockSpec((1, 1, blk), lambda h, i: (h, 0, 0)),
        ],
        out_specs=pl.BlockSpec((blk, HEAD_DIM), lambda h, i: (i, h)),
        out_shape=jax.ShapeDtypeStruct((s, n_heads * HEAD_DIM), bf16),
        scratch_shapes=[pltpu.VMEM((kaug_rows, 2 * HEAD_DIM), bf16), pltpu.VMEM((V7X_LANES, HEAD_DIM), f32),
                        pltpu.VMEM((V7X_SUBLANES, V7X_LANES), f32),
                        pltpu.VMEM((blk, 2 * HEAD_DIM), bf16), pltpu.VMEM((blk, 2 * HEAD_DIM), bf16),
                        pltpu.VMEM((blk, span), f32)],
        compiler_params=pltpu.CompilerParams(
            dimension_semantics=("arbitrary", "arbitrary"),
            vmem_limit_bytes=_vmem_limit(blocks + scratch)),
        name="moba_attn",
    )(qkv, qkv, qkv, qkv, slope_rows)


def _conv_ln_kernel(prev_ref, cur_ref, w_ref, cb_ref, g_ref, b_ref, o_ref, ext_ref, y_ref, *, tm, lane_chunk, row_chunk):
    i = pl.program_id(0)
    c = cur_ref.shape[1]
    ext_ref[0:CONV_HALO, :] = jnp.where(i > 0, prev_ref[...], 0.0)
    ext_ref[CONV_HALO:CONV_HALO + tm, :] = cur_ref[...]
    first = CONV_HALO - (CONV_KERNEL - 1)
    sub = V7X_SUBLANES
    for r0 in range(0, tm, row_chunk):
        for c0 in range(0, c, lane_chunk):
            acc = jnp.zeros((row_chunk, lane_chunk), f32)
            for rho in range(sub):
                offs = [o for o in range(first, first + CONV_KERNEL) if o % sub == rho]
                lo, hi = offs[0] - rho, offs[-1] - rho
                win = ext_ref[r0 + rho + lo:r0 + rho + hi + row_chunk, c0:c0 + lane_chunk]
                for o in offs:
                    k = o - first
                    wk = pltpu.repeat(w_ref[k, :, c0:c0 + lane_chunk], row_chunk // sub, axis=0)
                    acc = acc + wk * win[o - rho - lo:o - rho - lo + row_chunk]
            y_ref[r0:r0 + row_chunk, c0:c0 + lane_chunk] = acc + cb_ref[:, c0:c0 + lane_chunk]
    y = _layer_norm(y_ref[...], g_ref[...], b_ref[...])
    o_ref[...] = (y * jax.nn.sigmoid(y)).astype(o_ref.dtype)


def _conv_ln(hc, w, cb, g, b):
    s, c = hc.shape
    tm = _tile(s, 256)
    assert tm % CONV_HALO == 0 and CONV_HALO >= CONV_KERNEL - 1
    per = tm // CONV_HALO
    w = jnp.broadcast_to(w[:, None, :], (CONV_KERNEL, V7X_SUBLANES, c))
    blocks = 2 * (CONV_HALO * c * 4 + tm * c * 4 + CONV_KERNEL * V7X_SUBLANES * c * 4 + tm * c * 2)
    scratch = (tm + CONV_HALO) * c * 4 + tm * c * 4
    return pl.pallas_call(
        functools.partial(_conv_ln_kernel, tm=tm, lane_chunk=_tile(c, 256), row_chunk=_tile(tm, 64)),
        grid=(s // tm,),
        in_specs=[
            pl.BlockSpec((CONV_HALO, c), lambda i: (jnp.maximum(i * per - 1, 0), 0)),
            pl.BlockSpec((tm, c), lambda i: (i, 0)),
            pl.BlockSpec((CONV_KERNEL, V7X_SUBLANES, c), lambda i: (0, 0, 0)),
            pl.BlockSpec((1, c), lambda i: (0, 0)),
            pl.BlockSpec((1, c), lambda i: (0, 0)),
            pl.BlockSpec((1, c), lambda i: (0, 0)),
        ],
        out_specs=pl.BlockSpec((tm, c), lambda i: (i, 0)),
        out_shape=jax.ShapeDtypeStruct((s, c), bf16),
        scratch_shapes=[pltpu.VMEM((tm + CONV_HALO, c), f32), pltpu.VMEM((tm, c), f32)],
        compiler_params=pltpu.CompilerParams(
            dimension_semantics=("arbitrary",),
            vmem_limit_bytes=_vmem_limit(blocks + scratch)),
        name="conv_ln",
    )(hc, hc, w, cb, g, b)


def _merge_kernel(o_ref, hc_ref, ga_ref, gc_ref, wa_ref, wc_ref, m_ref):
    ya = jnp.dot(o_ref[...], wa_ref[...], preferred_element_type=f32)
    yc = jnp.dot(hc_ref[...], wc_ref[...], preferred_element_type=f32)
    m_ref[...] = (ga_ref[...].astype(f32) * ya + gc_ref[...].astype(f32) * yc).astype(m_ref.dtype)


def _merge(o, hc, gates, wa, wc):
    s, da = o.shape
    dc = hc.shape[1]
    d = wa.shape[1]
    tm = _tile(s, 1024)
    tn = _tile(d, 512)
    nj = d // tn
    blocks = 2 * (tm * da * 2 + tm * dc * 2 + 2 * tm * tn * 2 + da * tn * 2 + dc * tn * 2 + tm * tn * 2)
    return pl.pallas_call(
        _merge_kernel,
        grid=(s // tm, nj),
        in_specs=[
            pl.BlockSpec((tm, da), lambda i, j: (i, 0)),
            pl.BlockSpec((tm, dc), lambda i, j: (i, 0)),
            pl.BlockSpec((tm, tn), lambda i, j: (i, j)),
            pl.BlockSpec((tm, tn), lambda i, j: (i, nj + j)),
            _col_block_spec(da, tn),
            _col_block_spec(dc, tn),
        ],
        out_specs=pl.BlockSpec((tm, tn), lambda i, j: (i, j)),
        out_shape=jax.ShapeDtypeStruct((s, d), bf16),
        compiler_params=pltpu.CompilerParams(
            dimension_semantics=("arbitrary", "arbitrary"),
            vmem_limit_bytes=_vmem_limit(blocks)),
        name="merge",
    )(o, hc, gates, gates, _col_blocks(wa, tn), _col_blocks(wc, tn))


def _out_ln_kernel(m_ref, h_ref, w_ref, g_ref, b_ref, o_ref):
    mix = jnp.dot(m_ref[...], w_ref[...], preferred_element_type=f32)
    o_ref[...] = _layer_norm(DN_ALPHA * h_ref[...] + mix, g_ref[...], b_ref[...])


def _out_ln(m, h, w, g, b):
    s, d = h.shape
    tm = _tile(s, 512)
    blocks = 2 * (tm * d * 2 + tm * d * 4 + d * d * 2 + tm * d * 4)
    return pl.pallas_call(
        _out_ln_kernel,
        grid=(s // tm,),
        in_specs=[
            pl.BlockSpec((tm, d), lambda i: (i, 0)),
            pl.BlockSpec((tm, d), lambda i: (i, 0)),
            pl.BlockSpec((d, d), lambda i: (0, 0)),
            pl.BlockSpec((1, d), lambda i: (0, 0)),
            pl.BlockSpec((1, d), lambda i: (0, 0)),
        ],
        out_specs=pl.BlockSpec((tm, d), lambda i: (i, 0)),
        out_shape=jax.ShapeDtypeStruct((s, d), f32),
        compiler_params=pltpu.CompilerParams(
            dimension_semantics=("arbitrary",),
            vmem_limit_bytes=_vmem_limit(blocks)),
        name="out_ln",
    )(m, h, w, g, b)


def kernel(x, ln1_g, ln1_b, ffn1_wg, ffn1_wu, ffn1_wd, w_in, conv_dw, conv_db, conv_ln_g, conv_ln_b, w_conv_out, w_attn_out, w_out, ln2_g, ln2_b, ffn2_wg, ffn2_wu, ffn2_wd, ln3_g, ln3_b):
    batch, seq, d_model = x.shape
    depth = ffn1_wg.shape[0]
    assert depth == DEPTH
    attn_w = w_attn_out.shape[1]
    conv_w = w_conv_out.shape[1]
    n_heads = attn_w // HEAD_DIM
    c0, c1, c2 = 3 * attn_w, 3 * attn_w + conv_w, 3 * attn_w + 2 * conv_w
    row = lambda v: v.reshape(1, -1)

    outs = []
    for bi in range(batch):
        h = x[bi]
        for l in range(depth):
            h, hb = _ffn_ln(h, ffn1_wg[l].astype(bf16), ffn1_wu[l].astype(bf16), ffn1_wd[l].astype(bf16),
                            row(ln1_g[l]), row(ln1_b[l]))
            w = w_in[l]
            qkv = _proj(hb, w[:, :c0].astype(bf16), sigmoid=False, name="qkv_proj")
            glu = _glu_proj(hb, w[:, c0:c1].astype(bf16), w[:, c1:c2].astype(bf16))
            gates = _proj(hb, w[:, c2:].astype(bf16), sigmoid=True, name="gate_proj")
            o = _moba_attention(qkv, n_heads)
            hc = _conv_ln(glu, conv_dw[l].reshape(CONV_KERNEL, conv_w), row(conv_db[l]),
                          row(conv_ln_g[l]), row(conv_ln_b[l]))
            merged = _merge(o, hc, gates, w_attn_out[l].astype(bf16), w_conv_out[l].astype(bf16))
            h = _out_ln(merged, h, w_out[l].astype(bf16), row(ln2_g[l]), row(ln2_b[l]))
            h, _ = _ffn_ln(h, ffn2_wg[l].astype(bf16), ffn2_wu[l].astype(bf16), ffn2_wd[l].astype(bf16),
                           row(ln3_g[l]), row(ln3_b[l]))
        outs.append(h)
    return outs[0][None] if batch == 1 else jnp.stack(outs, axis=0)
```

```python
import functools

import numpy as np
import jax
import jax.numpy as jnp
from jax import lax
from jax.experimental import pallas as pl
from jax.experimental.pallas import tpu as pltpu

HEAD_DIM = 128
MOBA_BLOCK = 256
MOBA_TOPK = 3
MOBA_CHUNK = 8
EXP2_ZERO = 160.0
CONV_KERNEL = 31
LN_EPS = 1e-5
DEPTH = 1
DN_ALPHA = (2 * DEPTH) ** 0.25
ALIBI_MAX_LOG2 = 8.0
LOG2_E = 1.4426950408889634

V7X_VMEM_BYTES = 64 * 1024 * 1024
V7X_LANES = 128
V7X_SUBLANES = 8
CONV_HALO = 32
MASK_NEG = -1e30

f32 = jnp.float32
bf16 = jnp.bfloat16


def _vmem_limit(block_bytes):
    return int(min(V7X_VMEM_BYTES - 4 * 1024 * 1024, block_bytes + 16 * 1024 * 1024))


def _tile(dim, pref):
    if dim <= pref:
        return dim
    for t in range(pref, 0, -V7X_LANES):
        if dim % t == 0:
            return t
    raise ValueError((dim, pref))


def _layer_norm(y, g, b):
    mu = jnp.mean(y, axis=-1, keepdims=True)
    yc = y - mu
    var = jnp.mean(yc * yc, axis=-1, keepdims=True)
    return yc * lax.rsqrt(var + LN_EPS) * g + b


def _ffn_ln_kernel(x_ref, wg_ref, wu_ref, wd_ref, g_ref, b_ref, o_ref, ob_ref, xb_ref, *, ln_rows):
    f = pl.program_id(1)

    @pl.when(f == 0)
    def _():
        xb_ref[...] = x_ref[...].astype(bf16)
        o_ref[...] = jnp.zeros_like(o_ref)

    xb = xb_ref[...]
    gate = jnp.dot(xb, wg_ref[...], preferred_element_type=f32)
    up = jnp.dot(xb, wu_ref[...], preferred_element_type=f32)
    act = (gate * jax.nn.sigmoid(gate) * up).astype(bf16)
    o_ref[...] += jnp.dot(act, wd_ref[...], preferred_element_type=f32)

    @pl.when(f == pl.num_programs(1) - 1)
    def _():
        def body(r, carry):
            rows = pl.ds(pl.multiple_of(r * ln_rows, ln_rows), ln_rows)
            y = DN_ALPHA * x_ref[rows, :] + 0.5 * o_ref[rows, :]
            out = _layer_norm(y, g_ref[...], b_ref[...])
            o_ref[rows, :] = out
            ob_ref[rows, :] = out.astype(bf16)
            return carry

        lax.fori_loop(0, o_ref.shape[0] // ln_rows, body, 0)


def _ffn_ln(x, wg, wu, wd, g, b):
    s, d = x.shape
    ff = wg.shape[1]
    tm = _tile(s, 1024)
    tf = _tile(ff, 256)
    blocks = 2 * (tm * d * 4 + 2 * d * tf * 2 + tf * d * 2 + tm * d * 4 + tm * d * 2)
    scratch = tm * d * 2
    return pl.pallas_call(
        functools.partial(_ffn_ln_kernel, ln_rows=_tile(tm, 256)),
        grid=(s // tm, ff // tf),
        in_specs=[
            pl.BlockSpec((tm, d), lambda i, j: (i, 0)),
            pl.BlockSpec((d, tf), lambda i, j: (0, j)),
            pl.BlockSpec((d, tf), lambda i, j: (0, j)),
            pl.BlockSpec((tf, d), lambda i, j: (j, 0)),
            pl.BlockSpec((1, d), lambda i, j: (0, 0)),
            pl.BlockSpec((1, d), lambda i, j: (0, 0)),
        ],
        out_specs=[
            pl.BlockSpec((tm, d), lambda i, j: (i, 0)),
            pl.BlockSpec((tm, d), lambda i, j: (i, 0)),
        ],
        out_shape=[jax.ShapeDtypeStruct((s, d), f32), jax.ShapeDtypeStruct((s, d), bf16)],
        scratch_shapes=[pltpu.VMEM((tm, d), bf16)],
        compiler_params=pltpu.CompilerParams(
            dimension_semantics=("arbitrary", "arbitrary"),
            vmem_limit_bytes=_vmem_limit(blocks + scratch)),
        name="ffn_ln",
    )(x, wg, wu, wd, g, b)


def _proj_kernel(x_ref, w_ref, o_ref, *, sigmoid):
    y = jnp.dot(x_ref[...], w_ref[...], preferred_element_type=f32)
    if sigmoid:
        y = jax.nn.sigmoid(y)
    o_ref[...] = y.astype(o_ref.dtype)


def _proj(x, w, *, sigmoid, name):
    s, d = x.shape
    n = w.shape[1]
    tm = _tile(s, 1024)
    tn = _tile(n, 1024)
    blocks = 2 * (tm * d * 2 + d * tn * 2 + tm * tn * 2)
    return pl.pallas_call(
        functools.partial(_proj_kernel, sigmoid=sigmoid),
        grid=(s // tm, n // tn),
        in_specs=[
            pl.BlockSpec((tm, d), lambda i, j: (i, 0)),
            pl.BlockSpec((d, tn), lambda i, j: (0, j)),
        ],
        out_specs=pl.BlockSpec((tm, tn), lambda i, j: (i, j)),
        out_shape=jax.ShapeDtypeStruct((s, n), bf16),
        compiler_params=pltpu.CompilerParams(
            dimension_semantics=("arbitrary", "arbitrary"),
            vmem_limit_bytes=_vmem_limit(blocks)),
        name=name,
    )(x, w)


def _glu_proj_kernel(x_ref, wa_ref, wg_ref, o_ref):
    x = x_ref[...]
    a = jnp.dot(x, wa_ref[...], preferred_element_type=f32)
    g = jnp.dot(x, wg_ref[...], preferred_element_type=f32)
    o_ref[...] = a * jax.nn.sigmoid(g)


def _glu_proj(x, wa, wg):
    s, d = x.shape
    n = wa.shape[1]
    tm = _tile(s, 1024)
    tn = _tile(n, 512)
    blocks = 2 * (tm * d * 2 + 2 * d * tn * 2 + tm * tn * 4)
    return pl.pallas_call(
        _glu_proj_kernel,
        grid=(s // tm, n // tn),
        in_specs=[
            pl.BlockSpec((tm, d), lambda i, j: (i, 0)),
            pl.BlockSpec((d, tn), lambda i, j: (0, j)),
            pl.BlockSpec((d, tn), lambda i, j: (0, j)),
        ],
        out_specs=pl.BlockSpec((tm, tn), lambda i, j: (i, j)),
        out_shape=jax.ShapeDtypeStruct((s, n), f32),
        compiler_params=pltpu.CompilerParams(
            dimension_semantics=("arbitrary", "arbitrary"),
            vmem_limit_bytes=_vmem_limit(blocks)),
        name="glu_proj",
    )(x, wa, wg)


_NT = (((1,), (1,)), ((), ()))


def _moba_kernel(q_ref, qn_ref, k_ref, v_ref, slope_ref, o_ref, kaug_ref, kmean_ref, kmax_ref, qaug_ref, qnext_ref, s_ref, *, nb, chunk, scale):
    blk = MOBA_BLOCK
    span = chunk * blk
    pad = chunk - 1
    qb = pl.program_id(1)

    def make_qaug(q, n_past):
        gate = lax.dot_general(kmean_ref[...].astype(bf16), q, _NT, preferred_element_type=f32)
        row = lax.broadcasted_iota(jnp.int32, (V7X_LANES, blk), 0)
        neg_inf = f32(-jnp.inf)
        gate = jnp.where(row < n_past, gate, neg_inf)
        keep = jnp.zeros((V7X_LANES, blk), jnp.bool_)
        for r in range(min(MOBA_TOPK, nb)):
            top = jnp.max(gate, axis=0, keepdims=True)
            idx = jnp.min(jnp.where(gate == top, row, V7X_LANES), axis=0, keepdims=True)
            hit = row == idx
            keep = jnp.logical_or(keep, jnp.logical_and(hit, r < n_past))
            gate = jnp.where(hit, neg_inf, gate)
        maskbias = jnp.where(keep, 0.0, MASK_NEG).T.astype(bf16)
        return jnp.concatenate([(q.astype(f32) * scale).astype(bf16), maskbias], axis=1)

    @pl.when(qb == 0)
    def _build_head():
        lane = lax.broadcasted_iota(jnp.int32, (blk, V7X_LANES), 1)
        kmean_ref[...] = jnp.zeros_like(kmean_ref)
        dead = jnp.where(lane == V7X_LANES - 1, 1.0, 0.0).astype(bf16)
        for j in range(pad):
            kaug_ref[j * blk:(j + 1) * blk, :] = jnp.concatenate([jnp.zeros((blk, HEAD_DIM), bf16), dead], axis=1)

        def body(j, kmax):
            kj = k_ref[pl.ds(pl.multiple_of(j * blk, blk), blk), :]
            onehot = jnp.where(lane == j, 1.0, 0.0).astype(bf16)
            kaug_ref[pl.ds(pl.multiple_of((j + pad) * blk, blk), blk), :] = jnp.concatenate([kj, onehot], axis=1)
            kf = kj.astype(f32)
            kmean_ref[pl.ds(j, 1), :] = jnp.mean(kf, axis=0, keepdims=True)
            k2 = jnp.max(jnp.sum(kf * kf, axis=1, keepdims=True), axis=0, keepdims=True)
            return jnp.maximum(kmax, k2)

        kmax_ref[...] = jnp.broadcast_to(lax.fori_loop(0, nb, body, jnp.zeros((1, 1), f32)), kmax_ref.shape)
        qnext_ref[...] = make_qaug(q_ref[...], 0)

    qaug_ref[...] = qnext_ref[...]
    qnext_ref[...] = make_qaug(qn_ref[...], qb + 1)
    qs = (q_ref[...].astype(f32) * scale).astype(bf16)

    slope = slope_ref[0]
    cpos = lax.broadcasted_iota(jnp.int32, (1, blk), 1).astype(f32)
    half = blk // 2

    qf = qs.astype(f32)
    q2 = jnp.max(jnp.sum(qf * qf, axis=1, keepdims=True), axis=0, keepdims=True)
    smax = jnp.sqrt(q2 * kmax_ref[0:1, 0:1])
    live = jnp.floor((2.0 * smax + EXP2_ZERO) / (slope[:, 0:1] * span)) + 1.0
    live = jnp.max(jnp.minimum(live, float(nb)).astype(jnp.int32))
    n_chunks = jnp.minimum(lax.div(qb + (chunk - 1), chunk), live)

    def lane_fold(x, op):
        return op(x[:, :half], x[:, half:])

    d0 = pl.multiple_of(qb * blk, blk)
    s = lax.dot_general(qs, k_ref[pl.ds(d0, blk), :], _NT, preferred_element_type=f32)
    s = s + slope * cpos
    rr = lax.broadcasted_iota(jnp.int32, (blk, blk), 0)
    cc = lax.broadcasted_iota(jnp.int32, (blk, blk), 1)
    s = jnp.where(cc <= rr, s, MASK_NEG)
    m0 = jnp.max(s, axis=1, keepdims=True)
    p = jnp.exp2(s - m0)
    l0 = lane_fold(p, jnp.add)
    acc0 = jnp.dot(p.astype(bf16), v_ref[pl.ds(d0, blk), :], preferred_element_type=f32)

    def first_block(c):
        return qb - chunk * (c + 1)

    def score_pass(c):
        b0 = jnp.maximum(first_block(c) + pad, 0)
        mx = None
        for t in range(chunk):
            r0 = pl.multiple_of((b0 + t) * blk, blk)
            st = lax.dot_general(qaug_ref[...], kaug_ref[pl.ds(r0, blk), :], _NT,
                                 preferred_element_type=f32)
            dist = (chunk * (c + 1) - t) * blk
            st = st + slope * (cpos - jnp.asarray(dist, f32))
            s_ref[:, t * blk:(t + 1) * blk] = st
            e = lane_fold(st, jnp.maximum)
            mx = e if mx is None else jnp.maximum(mx, e)
        return jnp.max(mx, axis=1, keepdims=True)

    def value_pass(c, m_new):
        ls = jnp.zeros((blk, half), f32)
        pv = jnp.zeros((blk, HEAD_DIM), f32)
        for t in range(chunk):
            r0 = pl.multiple_of(jnp.maximum(first_block(c) + t, 0) * blk, blk)
            pt = jnp.exp2(s_ref[:, t * blk:(t + 1) * blk] - m_new)
            ls = ls + lane_fold(pt, jnp.add)
            pv = pv + jnp.dot(pt.astype(bf16), v_ref[pl.ds(r0, blk), :], preferred_element_type=f32)
        return ls, pv

    def body(c, carry):
        m, l, acc, mx_c = carry
        m_new = jnp.maximum(m, mx_c)
        alpha = jnp.exp2(m - m_new)
        ls, pv = value_pass(c, m_new)
        mx_next = score_pass(c + 1)
        return m_new, alpha * l + ls, alpha * acc + pv, mx_next

    mx0 = score_pass(0)

    m, l, acc, mx_c = lax.fori_loop(0, n_chunks - 1, body, (m0, l0, acc0, mx0))
    some = n_chunks > 0
    m_new = jnp.maximum(m, jnp.where(some, mx_c, m))
    alpha = jnp.exp2(m - m_new)
    ls, pv = value_pass(jnp.maximum(n_chunks - 1, 0), jnp.where(some, m_new, -MASK_NEG))
    l = alpha * l + ls
    acc = alpha * acc + pv
    o_ref[...] = (acc / jnp.sum(l, axis=1, keepdims=True)).astype(o_ref.dtype)


def _moba_attention(qkv, n_heads):
    s = qkv.shape[0]
    blk = MOBA_BLOCK
    assert s % blk == 0
    nb = s // blk
    assert nb < V7X_LANES
    chunk = min(MOBA_CHUNK, nb)
    span = chunk * blk
    kaug_rows = (nb + chunk - 1) * blk
    slopes = np.array([2.0 ** (-ALIBI_MAX_LOG2 * (h + 1) / n_heads) for h in range(n_heads)], np.float32)
    slopes = (slopes.astype(np.float64) * LOG2_E).astype(np.float32)
    slope_rows = jnp.asarray(np.broadcast_to(slopes[:, None, None], (n_heads, 1, blk)).copy())
    blocks = 2 * (2 * blk * HEAD_DIM * 2 + 2 * s * HEAD_DIM * 2 + blk * 4 + blk * HEAD_DIM * 2)
    scratch = (kaug_rows * 2 * HEAD_DIM * 2 + V7X_LANES * HEAD_DIM * 4 + V7X_SUBLANES * V7X_LANES * 4
               + 2 * blk * 2 * HEAD_DIM * 2 + blk * span * 4)
    return pl.pallas_call(
        functools.partial(_moba_kernel, nb=nb, chunk=chunk, scale=HEAD_DIM ** -0.5 * LOG2_E),
        grid=(n_heads, nb),
        in_specs=[
            pl.BlockSpec((blk, HEAD_DIM), lambda h, i: (i, h)),
            pl.BlockSpec((blk, HEAD_DIM), lambda h, i: (jnp.minimum(i + 1, nb - 1), h)),
            pl.BlockSpec((s, HEAD_DIM), lambda h, i: (0, n_heads + h)),
            pl.BlockSpec((s, HEAD_DIM), lambda h, i: (0, 2 * n_heads + h)),
            pl.BlockSpec((1, 1, blk), lambda h, i: (h, 0, 0)),
        ],
        out_specs=pl.BlockSpec((blk, HEAD_DIM), lambda h, i: (i, h)),
        out_shape=jax.ShapeDtypeStruct((s, n_heads * HEAD_DIM), bf16),
        scratch_shapes=[pltpu.VMEM((kaug_rows, 2 * HEAD_DIM), bf16), pltpu.VMEM((V7X_LANES, HEAD_DIM), f32),
                        pltpu.VMEM((V7X_SUBLANES, V7X_LANES), f32),
                        pltpu.VMEM((blk, 2 * HEAD_DIM), bf16), pltpu.VMEM((blk, 2 * HEAD_DIM), bf16),
                        pltpu.VMEM((blk, span), f32)],
        compiler_params=pltpu.CompilerParams(
            dimension_semantics=("arbitrary", "arbitrary"),
            vmem_limit_bytes=_vmem_limit(blocks + scratch)),
        name="moba_attn",
    )(qkv, qkv, qkv, qkv, slope_rows)


def _conv_ln_kernel(prev_ref, cur_ref, w_ref, cb_ref, g_ref, b_ref, o_ref, ext_ref, y_ref, *, tm, lane_chunk, row_chunk):
    i = pl.program_id(0)
    c = cur_ref.shape[1]
    ext_ref[0:CONV_HALO, :] = jnp.where(i > 0, prev_ref[...], 0.0)
    ext_ref[CONV_HALO:CONV_HALO + tm, :] = cur_ref[...]
    first = CONV_HALO - (CONV_KERNEL - 1)
    sub = V7X_SUBLANES
    for r0 in range(0, tm, row_chunk):
        for c0 in range(0, c, lane_chunk):
            acc = jnp.zeros((row_chunk, lane_chunk), f32)
            for rho in range(sub):
                offs = [o for o in range(first, first + CONV_KERNEL) if o % sub == rho]
                lo, hi = offs[0] - rho, offs[-1] - rho
                win = ext_ref[r0 + rho + lo:r0 + rho + hi + row_chunk, c0:c0 + lane_chunk]
                for o in offs:
                    k = o - first
                    wk = jnp.tile(w_ref[k, :, c0:c0 + lane_chunk], (row_chunk // sub, 1))
                    acc = acc + wk * win[o - rho - lo:o - rho - lo + row_chunk]
            y_ref[r0:r0 + row_chunk, c0:c0 + lane_chunk] = acc + cb_ref[:, c0:c0 + lane_chunk]
    y = _layer_norm(y_ref[...], g_ref[...], b_ref[...])
    o_ref[...] = (y * jax.nn.sigmoid(y)).astype(o_ref.dtype)


def _conv_ln(hc, w, cb, g, b):
    s, c = hc.shape
    tm = _tile(s, 256)
    assert tm % CONV_HALO == 0 and CONV_HALO >= CONV_KERNEL - 1
    per = tm // CONV_HALO
    w = jnp.broadcast_to(w[:, None, :], (CONV_KERNEL, V7X_SUBLANES, c))
    blocks = 2 * (CONV_HALO * c * 4 + tm * c * 4 + CONV_KERNEL * V7X_SUBLANES * c * 4 + tm * c * 2)
    scratch = (tm + CONV_HALO) * c * 4 + tm * c * 4
    return pl.pallas_call(
        functools.partial(_conv_ln_kernel, tm=tm, lane_chunk=_tile(c, 256), row_chunk=_tile(tm, 64)),
        grid=(s // tm,),
        in_specs=[
            pl.BlockSpec((CONV_HALO, c), lambda i: (jnp.maximum(i * per - 1, 0), 0)),
            pl.BlockSpec((tm, c), lambda i: (i, 0)),
            pl.BlockSpec((CONV_KERNEL, V7X_SUBLANES, c), lambda i: (0, 0, 0)),
            pl.BlockSpec((1, c), lambda i: (0, 0)),
            pl.BlockSpec((1, c), lambda i: (0, 0)),
            pl.BlockSpec((1, c), lambda i: (0, 0)),
        ],
        out_specs=pl.BlockSpec((tm, c), lambda i: (i, 0)),
        out_shape=jax.ShapeDtypeStruct((s, c), bf16),
        scratch_shapes=[pltpu.VMEM((tm + CONV_HALO, c), f32), pltpu.VMEM((tm, c), f32)],
        compiler_params=pltpu.CompilerParams(
            dimension_semantics=("arbitrary",),
            vmem_limit_bytes=_vmem_limit(blocks + scratch)),
        name="conv_ln",
    )(hc, hc, w, cb, g, b)


def _merge_kernel(o_ref, hc_ref, ga_ref, gc_ref, wa_ref, wc_ref, m_ref):
    ya = jnp.dot(o_ref[...], wa_ref[...], preferred_element_type=f32)
    yc = jnp.dot(hc_ref[...], wc_ref[...], preferred_element_type=f32)
    m_ref[...] = (ga_ref[...].astype(f32) * ya + gc_ref[...].astype(f32) * yc).astype(m_ref.dtype)


def _merge(o, hc, gates, wa, wc):
    s, da = o.shape
    dc = hc.shape[1]
    d = wa.shape[1]
    tm = _tile(s, 1024)
    tn = _tile(d, 512)
    nj = d // tn
    blocks = 2 * (tm * da * 2 + tm * dc * 2 + 2 * tm * tn * 2 + da * tn * 2 + dc * tn * 2 + tm * tn * 2)
    return pl.pallas_call(
        _merge_kernel,
        grid=(s // tm, nj),
        in_specs=[
            pl.BlockSpec((tm, da), lambda i, j: (i, 0)),
            pl.BlockSpec((tm, dc), lambda i, j: (i, 0)),
            pl.BlockSpec((tm, tn), lambda i, j: (i, j)),
            pl.BlockSpec((tm, tn), lambda i, j: (i, nj + j)),
            pl.BlockSpec((da, tn), lambda i, j: (0, j)),
            pl.BlockSpec((dc, tn), lambda i, j: (0, j)),
        ],
        out_specs=pl.BlockSpec((tm, tn), lambda i, j: (i, j)),
        out_shape=jax.ShapeDtypeStruct((s, d), bf16),
        compiler_params=pltpu.CompilerParams(
            dimension_semantics=("arbitrary", "arbitrary"),
            vmem_limit_bytes=_vmem_limit(blocks)),
        name="merge",
    )(o, hc, gates, gates, wa, wc)


def _out_ln_kernel(m_ref, h_ref, w_ref, g_ref, b_ref, o_ref):
    mix = jnp.dot(m_ref[...], w_ref[...], preferred_element_type=f32)
    o_ref[...] = _layer_norm(DN_ALPHA * h_ref[...] + mix, g_ref[...], b_ref[...])


def _out_ln(m, h, w, g, b):
    s, d = h.shape
    tm = _tile(s, 512)
    blocks = 2 * (tm * d * 2 + tm * d * 4 + d * d * 2 + tm * d * 4)
    return pl.pallas_call(
        _out_ln_kernel,
        grid=(s // tm,),
        in_specs=[
            pl.BlockSpec((tm, d), lambda i: (i, 0)),
            pl.BlockSpec((tm, d), lambda i: (i, 0)),
            pl.BlockSpec((d, d), lambda i: (0, 0)),
            pl.BlockSpec((1, d), lambda i: (0, 0)),
            pl.BlockSpec((1, d), lambda i: (0, 0)),
        ],
        out_specs=pl.BlockSpec((tm, d), lambda i: (i, 0)),
        out_shape=jax.ShapeDtypeStruct((s, d), f32),
        compiler_params=pltpu.CompilerParams(
            dimension_semantics=("arbitrary",),
            vmem_limit_bytes=_vmem_limit(blocks)),
        name="out_ln",
    )(m, h, w, g, b)


def kernel(x, ln1_g, ln1_b, ffn1_wg, ffn1_wu, ffn1_wd, w_in, conv_dw, conv_db, conv_ln_g, conv_ln_b, w_conv_out, w_attn_out, w_out, ln2_g, ln2_b, ffn2_wg, ffn2_wu, ffn2_wd, ln3_g, ln3_b):
    batch, seq, d_model = x.shape
    depth = ffn1_wg.shape[0]
    assert depth == DEPTH
    attn_w = w_attn_out.shape[1]
    conv_w = w_conv_out.shape[1]
    n_heads = attn_w // HEAD_DIM
    c0, c1, c2 = 3 * attn_w, 3 * attn_w + conv_w, 3 * attn_w + 2 * conv_w
    row = lambda v: v.reshape(1, -1)

    outs = []
    for bi in range(batch):
        h = x[bi]
        for l in range(depth):
            h, hb = _ffn_ln(h, ffn1_wg[l].astype(bf16), ffn1_wu[l].astype(bf16), ffn1_wd[l].astype(bf16),
                            row(ln1_g[l]), row(ln1_b[l]))
            w = w_in[l]
            qkv = _proj(hb, w[:, :c0].astype(bf16), sigmoid=False, name="qkv_proj")
            glu = _glu_proj(hb, w[:, c0:c1].astype(bf16), w[:, c1:c2].astype(bf16))
            gates = _proj(hb, w[:, c2:].astype(bf16), sigmoid=True, name="gate_proj")
            o = _moba_attention(qkv, n_heads)
            hc = _conv_ln(glu, conv_dw[l].reshape(CONV_KERNEL, conv_w), row(conv_db[l]),
                          row(conv_ln_g[l]), row(conv_ln_b[l]))
            merged = _merge(o, hc, gates, w_attn_out[l].astype(bf16), w_conv_out[l].astype(bf16))
            h = _out_ln(merged, h, w_out[l].astype(bf16), row(ln2_g[l]), row(ln2_b[l]))
            h, _ = _ffn_ln(h, ffn2_wg[l].astype(bf16), ffn2_wu[l].astype(bf16), ffn2_wd[l].astype(bf16),
                           row(ln3_g[l]), row(ln3_b[l]))
        outs.append(h)
    return outs[0][None] if batch == 1 else jnp.stack(outs, axis=0)
```

```python
import functools

import numpy as np
import jax
import jax.numpy as jnp
from jax import lax
from jax.experimental import pallas as pl
from jax.experimental.pallas import tpu as pltpu

HEAD_DIM = 128
MOBA_BLOCK = 256
MOBA_TOPK = 3
MOBA_CHUNK = 8
MOBA_PAIR = 4
EXP2_ZERO = 160.0
CONV_KERNEL = 31
LN_EPS = 1e-5
DEPTH = 1
DN_ALPHA = (2 * DEPTH) ** 0.25
ALIBI_MAX_LOG2 = 8.0
LOG2_E = 1.4426950408889634

V7X_VMEM_BYTES = 64 * 1024 * 1024
V7X_LANES = 128
V7X_SUBLANES = 8
CONV_HALO = 32
MASK_NEG = -1e30

f32 = jnp.float32
bf16 = jnp.bfloat16


def _vmem_limit(block_bytes):
    return int(min(V7X_VMEM_BYTES - 4 * 1024 * 1024, block_bytes + 16 * 1024 * 1024))


def _tile(dim, pref):
    if dim <= pref:
        return dim
    for t in range(pref, 0, -V7X_LANES):
        if dim % t == 0:
            return t
    raise ValueError((dim, pref))


def _layer_norm(y, g, b):
    mu = jnp.mean(y, axis=-1, keepdims=True)
    yc = y - mu
    var = jnp.mean(yc * yc, axis=-1, keepdims=True)
    return yc * lax.rsqrt(var + LN_EPS) * g + b


def _ffn_ln_kernel(x_ref, wg_ref, wu_ref, wd_ref, g_ref, b_ref, o_ref, ob_ref, xb_ref, *, ln_rows):
    f = pl.program_id(1)

    @pl.when(f == 0)
    def _():
        xb_ref[...] = x_ref[...].astype(bf16)
        o_ref[...] = jnp.zeros_like(o_ref)

    xb = xb_ref[...]
    gate = jnp.dot(xb, wg_ref[...], preferred_element_type=f32)
    up = jnp.dot(xb, wu_ref[...], preferred_element_type=f32)
    act = (gate * jax.nn.sigmoid(gate) * up).astype(bf16)
    o_ref[...] += jnp.dot(act, wd_ref[...], preferred_element_type=f32)

    @pl.when(f == pl.num_programs(1) - 1)
    def _():
        def body(r, carry):
            rows = pl.ds(pl.multiple_of(r * ln_rows, ln_rows), ln_rows)
            y = DN_ALPHA * x_ref[rows, :] + 0.5 * o_ref[rows, :]
            out = _layer_norm(y, g_ref[...], b_ref[...])
            o_ref[rows, :] = out
            ob_ref[rows, :] = out.astype(bf16)
            return carry

        lax.fori_loop(0, o_ref.shape[0] // ln_rows, body, 0)


def _ffn_ln(x, wg, wu, wd, g, b):
    s, d = x.shape
    ff = wg.shape[1]
    tm = _tile(s, 1024)
    tf = _tile(ff, 256)
    blocks = 2 * (tm * d * 4 + 2 * d * tf * 2 + tf * d * 2 + tm * d * 4 + tm * d * 2)
    scratch = tm * d * 2
    return pl.pallas_call(
        functools.partial(_ffn_ln_kernel, ln_rows=_tile(tm, 256)),
        grid=(s // tm, ff // tf),
        in_specs=[
            pl.BlockSpec((tm, d), lambda i, j: (i, 0)),
            pl.BlockSpec((d, tf), lambda i, j: (0, j)),
            pl.BlockSpec((d, tf), lambda i, j: (0, j)),
            pl.BlockSpec((tf, d), lambda i, j: (j, 0)),
            pl.BlockSpec((1, d), lambda i, j: (0, 0)),
            pl.BlockSpec((1, d), lambda i, j: (0, 0)),
        ],
        out_specs=[
            pl.BlockSpec((tm, d), lambda i, j: (i, 0)),
            pl.BlockSpec((tm, d), lambda i, j: (i, 0)),
        ],
        out_shape=[jax.ShapeDtypeStruct((s, d), f32), jax.ShapeDtypeStruct((s, d), bf16)],
        scratch_shapes=[pltpu.VMEM((tm, d), bf16)],
        compiler_params=pltpu.CompilerParams(
            dimension_semantics=("arbitrary", "arbitrary"),
            vmem_limit_bytes=_vmem_limit(blocks + scratch)),
        name="ffn_ln",
    )(x, wg, wu, wd, g, b)


def _proj_kernel(x_ref, w_ref, o_ref, *, sigmoid):
    y = jnp.dot(x_ref[...], w_ref[...], preferred_element_type=f32)
    if sigmoid:
        y = jax.nn.sigmoid(y)
    o_ref[...] = y.astype(o_ref.dtype)


def _proj(x, w, *, sigmoid, name):
    s, d = x.shape
    n = w.shape[1]
    tm = _tile(s, 1024)
    tn = _tile(n, 1024)
    blocks = 2 * (tm * d * 2 + d * tn * 2 + tm * tn * 2)
    return pl.pallas_call(
        functools.partial(_proj_kernel, sigmoid=sigmoid),
        grid=(s // tm, n // tn),
        in_specs=[
            pl.BlockSpec((tm, d), lambda i, j: (i, 0)),
            pl.BlockSpec((d, tn), lambda i, j: (0, j)),
        ],
        out_specs=pl.BlockSpec((tm, tn), lambda i, j: (i, j)),
        out_shape=jax.ShapeDtypeStruct((s, n), bf16),
        compiler_params=pltpu.CompilerParams(
            dimension_semantics=("arbitrary", "arbitrary"),
            vmem_limit_bytes=_vmem_limit(blocks)),
        name=name,
    )(x, w)


def _glu_proj_kernel(x_ref, wa_ref, wg_ref, o_ref):
    x = x_ref[...]
    a = jnp.dot(x, wa_ref[...], preferred_element_type=f32)
    g = jnp.dot(x, wg_ref[...], preferred_element_type=f32)
    o_ref[...] = a * jax.nn.sigmoid(g)


def _glu_proj(x, wa, wg):
    s, d = x.shape
    n = wa.shape[1]
    tm = _tile(s, 1024)
    tn = _tile(n, 1024)
    blocks = 2 * (tm * d * 2 + 2 * d * tn * 2 + tm * tn * 4)
    return pl.pallas_call(
        _glu_proj_kernel,
        grid=(s // tm, n // tn),
        in_specs=[
            pl.BlockSpec((tm, d), lambda i, j: (i, 0)),
            pl.BlockSpec((d, tn), lambda i, j: (0, j)),
            pl.BlockSpec((d, tn), lambda i, j: (0, j)),
        ],
        out_specs=pl.BlockSpec((tm, tn), lambda i, j: (i, j)),
        out_shape=jax.ShapeDtypeStruct((s, n), f32),
        compiler_params=pltpu.CompilerParams(
            dimension_semantics=("arbitrary", "arbitrary"),
            vmem_limit_bytes=_vmem_limit(blocks)),
        name="glu_proj",
    )(x, wa, wg)


_NT = (((1,), (1,)), ((), ()))


def _moba_kernel(q_ref, qn_ref, k_ref, v_ref, slope_ref, o_ref, kaug_ref, kmean_ref, kmax_ref, qaug_ref, qnext_ref, s_ref, *, nb, chunk, scale):
    blk = MOBA_BLOCK
    span = chunk * blk
    pad = chunk - 1
    step = pl.program_id(1)
    qbs = [step + u * (nb // MOBA_PAIR) for u in range(MOBA_PAIR)]

    def make_qaug(q, n_past):
        gate = lax.dot_general(kmean_ref[...].astype(bf16), q, _NT, preferred_element_type=f32)
        row = lax.broadcasted_iota(jnp.int32, (V7X_LANES, blk), 0)
        neg_inf = f32(-jnp.inf)
        gate = jnp.where(row < n_past, gate, neg_inf)
        keep = jnp.zeros((V7X_LANES, blk), jnp.bool_)
        for r in range(min(MOBA_TOPK, nb)):
            top = jnp.max(gate, axis=0, keepdims=True)
            idx = jnp.min(jnp.where(gate == top, row, V7X_LANES), axis=0, keepdims=True)
            hit = row == idx
            keep = jnp.logical_or(keep, jnp.logical_and(hit, r < n_past))
            gate = jnp.where(hit, neg_inf, gate)
        maskbias = jnp.where(keep, 0.0, MASK_NEG).T.astype(bf16)
        return jnp.concatenate([(q.astype(f32) * scale).astype(bf16), maskbias], axis=1)

    @pl.when(step == 0)
    def _build_head():
        lane = lax.broadcasted_iota(jnp.int32, (blk, V7X_LANES), 1)
        kmean_ref[...] = jnp.zeros_like(kmean_ref)
        dead = jnp.where(lane == V7X_LANES - 1, 1.0, 0.0).astype(bf16)
        for j in range(pad):
            kaug_ref[j * blk:(j + 1) * blk, :] = jnp.concatenate([jnp.zeros((blk, HEAD_DIM), bf16), dead], axis=1)

        def body(j, kmax):
            kj = k_ref[pl.ds(pl.multiple_of(j * blk, blk), blk), :]
            onehot = jnp.where(lane == j, 1.0, 0.0).astype(bf16)
            kaug_ref[pl.ds(pl.multiple_of((j + pad) * blk, blk), blk), :] = jnp.concatenate([kj, onehot], axis=1)
            kf = kj.astype(f32)
            kmean_ref[pl.ds(j, 1), :] = jnp.mean(kf, axis=0, keepdims=True)
            k2 = jnp.max(jnp.sum(kf * kf, axis=1, keepdims=True), axis=0, keepdims=True)
            return jnp.maximum(kmax, k2)

        kmax_ref[...] = jnp.broadcast_to(lax.fori_loop(0, nb, body, jnp.zeros((1, 1), f32)), kmax_ref.shape)
        for u in range(MOBA_PAIR):
            qnext_ref[u] = make_qaug(q_ref[u], qbs[u])

    qaug_ref[...] = qnext_ref[...]
    for u in range(MOBA_PAIR):
        qnext_ref[u] = make_qaug(qn_ref[u], qbs[u] + 1)

    slope = slope_ref[0]
    cpos = lax.broadcasted_iota(jnp.int32, (1, blk), 1).astype(f32)
    half = blk // 2

    def lane_fold(x, op):
        return op(x[:, :half], x[:, half:])

    def first_block(u, c):
        return qbs[u] - chunk * (c + 1)

    def score_pass(u, c):
        b0 = jnp.maximum(first_block(u, c) + pad, 0)
        mx = None
        for t in range(chunk):
            r0 = pl.multiple_of((b0 + t) * blk, blk)
            st = lax.dot_general(qaug_ref[u], kaug_ref[pl.ds(r0, blk), :], _NT,
                                 preferred_element_type=f32)
            dist = (chunk * (c + 1) - t) * blk
            st = st + slope * (cpos - jnp.asarray(dist, f32))
            s_ref[u, :, t * blk:(t + 1) * blk] = st
            e = lane_fold(st, jnp.maximum)
            mx = e if mx is None else jnp.maximum(mx, e)
        return jnp.max(mx, axis=1, keepdims=True)

    def value_pass(u, c, m_new):
        ls = jnp.zeros((blk, half), f32)
        pv = jnp.zeros((blk, HEAD_DIM), f32)
        for t in range(chunk):
            r0 = pl.multiple_of(jnp.maximum(first_block(u, c) + t, 0) * blk, blk)
            pt = jnp.exp2(s_ref[u, :, t * blk:(t + 1) * blk] - m_new)
            ls = ls + lane_fold(pt, jnp.add)
            pv = pv + jnp.dot(pt.astype(bf16), v_ref[pl.ds(r0, blk), :], preferred_element_type=f32)
        return ls, pv

    def prologue(u):
        qb = qbs[u]
        qs = (q_ref[u].astype(f32) * scale).astype(bf16)
        qf = qs.astype(f32)
        q2 = jnp.max(jnp.sum(qf * qf, axis=1, keepdims=True), axis=0, keepdims=True)
        smax = jnp.sqrt(q2 * kmax_ref[0:1, 0:1])
        live = jnp.floor((2.0 * smax + EXP2_ZERO) / (slope[:, 0:1] * span)) + 1.0
        live = jnp.max(jnp.minimum(live, float(nb)).astype(jnp.int32))
        n_chunks = jnp.minimum(lax.div(qb + (chunk - 1), chunk), live)
        d0 = pl.multiple_of(qb * blk, blk)
        s = lax.dot_general(qs, k_ref[pl.ds(d0, blk), :], _NT, preferred_element_type=f32)
        s = s + slope * cpos
        rr = lax.broadcasted_iota(jnp.int32, (blk, blk), 0)
        cc = lax.broadcasted_iota(jnp.int32, (blk, blk), 1)
        s = jnp.where(cc <= rr, s, MASK_NEG)
        m0 = jnp.max(s, axis=1, keepdims=True)
        p = jnp.exp2(s - m0)
        l0 = lane_fold(p, jnp.add)
        acc0 = jnp.dot(p.astype(bf16), v_ref[pl.ds(d0, blk), :], preferred_element_type=f32)
        return n_chunks, (m0, l0, acc0, score_pass(u, 0))

    def chunk_loop(u, n_chunks, init):
        def body(c, carry):
            m, l, acc, mx_c = carry
            m_new = jnp.maximum(m, mx_c)
            alpha = jnp.exp2(m - m_new)
            ls, pv = value_pass(u, c, m_new)
            mx_next = score_pass(u, c + 1)
            return m_new, alpha * l + ls, alpha * acc + pv, mx_next

        return lax.fori_loop(0, n_chunks - 1, body, init)

    def last_pass(u, n_chunks, state):
        m, l, acc, mx_c = state
        some = n_chunks > 0
        m_new = jnp.maximum(m, jnp.where(some, mx_c, m))
        alpha = jnp.exp2(m - m_new)
        ls, pv = value_pass(u, jnp.maximum(n_chunks - 1, 0), jnp.where(some, m_new, -MASK_NEG))
        l = alpha * l + ls
        acc = alpha * acc + pv
        o_ref[u] = (acc / jnp.sum(l, axis=1, keepdims=True)).astype(o_ref.dtype)

    starts = [prologue(u) for u in range(MOBA_PAIR)]
    states = [chunk_loop(u, n, init) for u, (n, init) in enumerate(starts)]
    for u in range(MOBA_PAIR):
        last_pass(u, starts[u][0], states[u])


def _moba_attention(qkv, n_heads):
    s = qkv.shape[0]
    blk = MOBA_BLOCK
    assert s % blk == 0
    nb = s // blk
    assert nb < V7X_LANES
    chunk = min(MOBA_CHUNK, nb)
    span = chunk * blk
    kaug_rows = (nb + chunk - 1) * blk
    slopes = np.array([2.0 ** (-ALIBI_MAX_LOG2 * (h + 1) / n_heads) for h in range(n_heads)], np.float32)
    slopes = (slopes.astype(np.float64) * LOG2_E).astype(np.float32)
    slope_rows = jnp.asarray(np.broadcast_to(slopes[:, None, None], (n_heads, 1, blk)).copy())
    pair = MOBA_PAIR
    assert nb % pair == 0
    steps = nb // pair
    blocks = 2 * (2 * pair * blk * HEAD_DIM * 2 + 2 * s * HEAD_DIM * 2 + blk * 4 + pair * blk * HEAD_DIM * 2)
    scratch = (kaug_rows * 2 * HEAD_DIM * 2 + V7X_LANES * HEAD_DIM * 4 + V7X_SUBLANES * V7X_LANES * 4
               + 2 * pair * blk * 2 * HEAD_DIM * 2 + pair * blk * span * 4)
    q_view = qkv.reshape(pair, s // pair, qkv.shape[1])
    out = pl.pallas_call(
        functools.partial(_moba_kernel, nb=nb, chunk=chunk, scale=HEAD_DIM ** -0.5 * LOG2_E),
        grid=(n_heads, steps),
        in_specs=[
            pl.BlockSpec((pair, blk, HEAD_DIM), lambda h, i: (0, i, h)),
            pl.BlockSpec((pair, blk, HEAD_DIM), lambda h, i: (0, jnp.minimum(i + 1, steps - 1), h)),
            pl.BlockSpec((s, HEAD_DIM), lambda h, i: (0, n_heads + h)),
            pl.BlockSpec((s, HEAD_DIM), lambda h, i: (0, 2 * n_heads + h)),
            pl.BlockSpec((1, 1, blk), lambda h, i: (h, 0, 0)),
        ],
        out_specs=pl.BlockSpec((pair, blk, HEAD_DIM), lambda h, i: (0, i, h)),
        out_shape=jax.ShapeDtypeStruct((pair, s // pair, n_heads * HEAD_DIM), bf16),
        scratch_shapes=[pltpu.VMEM((kaug_rows, 2 * HEAD_DIM), bf16), pltpu.VMEM((V7X_LANES, HEAD_DIM), f32),
                        pltpu.VMEM((V7X_SUBLANES, V7X_LANES), f32),
                        pltpu.VMEM((pair, blk, 2 * HEAD_DIM), bf16), pltpu.VMEM((pair, blk, 2 * HEAD_DIM), bf16),
                        pltpu.VMEM((pair, blk, span), f32)],
        compiler_params=pltpu.CompilerParams(
            dimension_semantics=("arbitrary", "arbitrary"),
            vmem_limit_bytes=_vmem_limit(blocks + scratch)),
        name="moba_attn",
    )(q_view, q_view, qkv, qkv, slope_rows)
    return out.reshape(s, n_heads * HEAD_DIM)


def _conv_ln_kernel(prev_ref, cur_ref, w_ref, cb_ref, g_ref, b_ref, o_ref, ext_ref, y_ref, *, tm, lane_chunk, row_chunk):
    i = pl.program_id(0)
    c = cur_ref.shape[1]
    ext_ref[0:CONV_HALO, :] = jnp.where(i > 0, prev_ref[...], 0.0)
    ext_ref[CONV_HALO:CONV_HALO + tm, :] = cur_ref[...]
    first = CONV_HALO - (CONV_KERNEL - 1)
    sub = V7X_SUBLANES
    for r0 in range(0, tm, row_chunk):
        for c0 in range(0, c, lane_chunk):
            acc = jnp.zeros((row_chunk, lane_chunk), f32)
            for rho in range(sub):
                offs = [o for o in range(first, first + CONV_KERNEL) if o % sub == rho]
                lo, hi = offs[0] - rho, offs[-1] - rho
                win = ext_ref[r0 + rho + lo:r0 + rho + hi + row_chunk, c0:c0 + lane_chunk]
                for o in offs:
                    k = o - first
                    wk = jnp.tile(w_ref[k, :, c0:c0 + lane_chunk], (row_chunk // sub, 1))
                    acc = acc + wk * win[o - rho - lo:o - rho - lo + row_chunk]
            y_ref[r0:r0 + row_chunk, c0:c0 + lane_chunk] = acc + cb_ref[:, c0:c0 + lane_chunk]
    y = _layer_norm(y_ref[...], g_ref[...], b_ref[...])
    o_ref[...] = (y * jax.nn.sigmoid(y)).astype(o_ref.dtype)


def _conv_ln(hc, w, cb, g, b):
    s, c = hc.shape
    tm = _tile(s, 256)
    assert tm % CONV_HALO == 0 and CONV_HALO >= CONV_KERNEL - 1
    per = tm // CONV_HALO
    w = jnp.broadcast_to(w[:, None, :], (CONV_KERNEL, V7X_SUBLANES, c))
    blocks = 2 * (CONV_HALO * c * 4 + tm * c * 4 + CONV_KERNEL * V7X_SUBLANES * c * 4 + tm * c * 2)
    scratch = (tm + CONV_HALO) * c * 4 + tm * c * 4
    return pl.pallas_call(
        functools.partial(_conv_ln_kernel, tm=tm, lane_chunk=_tile(c, 256), row_chunk=_tile(tm, 64)),
        grid=(s // tm,),
        in_specs=[
            pl.BlockSpec((CONV_HALO, c), lambda i: (jnp.maximum(i * per - 1, 0), 0)),
            pl.BlockSpec((tm, c), lambda i: (i, 0)),
            pl.BlockSpec((CONV_KERNEL, V7X_SUBLANES, c), lambda i: (0, 0, 0)),
            pl.BlockSpec((1, c), lambda i: (0, 0)),
            pl.BlockSpec((1, c), lambda i: (0, 0)),
            pl.BlockSpec((1, c), lambda i: (0, 0)),
        ],
        out_specs=pl.BlockSpec((tm, c), lambda i: (i, 0)),
        out_shape=jax.ShapeDtypeStruct((s, c), bf16),
        scratch_shapes=[pltpu.VMEM((tm + CONV_HALO, c), f32), pltpu.VMEM((tm, c), f32)],
        compiler_params=pltpu.CompilerParams(
            dimension_semantics=("arbitrary",),
            vmem_limit_bytes=_vmem_limit(blocks + scratch)),
        name="conv_ln",
    )(hc, hc, w, cb, g, b)


def _merge_kernel(o_ref, hc_ref, ga_ref, gc_ref, wa_ref, wc_ref, m_ref):
    ya = jnp.dot(o_ref[...], wa_ref[...], preferred_element_type=f32)
    yc = jnp.dot(hc_ref[...], wc_ref[...], preferred_element_type=f32)
    m_ref[...] = (ga_ref[...].astype(f32) * ya + gc_ref[...].astype(f32) * yc).astype(m_ref.dtype)


def _merge(o, hc, gates, wa, wc):
    s, da = o.shape
    dc = hc.shape[1]
    d = wa.shape[1]
    tm = _tile(s, 1024)
    tn = _tile(d, 1024)
    nj = d // tn
    blocks = 2 * (tm * da * 2 + tm * dc * 2 + 2 * tm * tn * 2 + da * tn * 2 + dc * tn * 2 + tm * tn * 2)
    return pl.pallas_call(
        _merge_kernel,
        grid=(s // tm, nj),
        in_specs=[
            pl.BlockSpec((tm, da), lambda i, j: (i, 0)),
            pl.BlockSpec((tm, dc), lambda i, j: (i, 0)),
            pl.BlockSpec((tm, tn), lambda i, j: (i, j)),
            pl.BlockSpec((tm, tn), lambda i, j: (i, nj + j)),
            pl.BlockSpec((da, tn), lambda i, j: (0, j)),
            pl.BlockSpec((dc, tn), lambda i, j: (0, j)),
        ],
        out_specs=pl.BlockSpec((tm, tn), lambda i, j: (i, j)),
        out_shape=jax.ShapeDtypeStruct((s, d), bf16),
        compiler_params=pltpu.CompilerParams(
            dimension_semantics=("arbitrary", "arbitrary"),
            vmem_limit_bytes=_vmem_limit(blocks)),
        name="merge",
    )(o, hc, gates, gates, wa, wc)


def _out_ln_kernel(m_ref, h_ref, w_ref, g_ref, b_ref, o_ref):
    mix = jnp.dot(m_ref[...], w_ref[...], preferred_element_type=f32)
    o_ref[...] = _layer_norm(DN_ALPHA * h_ref[...] + mix, g_ref[...], b_ref[...])


def _out_ln(m, h, w, g, b):
    s, d = h.shape
    tm = _tile(s, 512)
    blocks = 2 * (tm * d * 2 + tm * d * 4 + d * d * 2 + tm * d * 4)
    return pl.pallas_call(
        _out_ln_kernel,
        grid=(s // tm,),
        in_specs=[
            pl.BlockSpec((tm, d), lambda i: (i, 0)),
            pl.BlockSpec((tm, d), lambda i: (i, 0)),
            pl.BlockSpec((d, d), lambda i: (0, 0)),
            pl.BlockSpec((1, d), lambda i: (0, 0)),
            pl.BlockSpec((1, d), lambda i: (0, 0)),
        ],
        out_specs=pl.BlockSpec((tm, d), lambda i: (i, 0)),
        out_shape=jax.ShapeDtypeStruct((s, d), f32),
        compiler_params=pltpu.CompilerParams(
            dimension_semantics=("arbitrary",),
            vmem_limit_bytes=_vmem_limit(blocks)),
        name="out_ln",
    )(m, h, w, g, b)


def kernel(x, ln1_g, ln1_b, ffn1_wg, ffn1_wu, ffn1_wd, w_in, conv_dw, conv_db, conv_ln_g, conv_ln_b, w_conv_out, w_attn_out, w_out, ln2_g, ln2_b, ffn2_wg, ffn2_wu, ffn2_wd, ln3_g, ln3_b):
    batch, seq, d_model = x.shape
    depth = ffn1_wg.shape[0]
    assert depth == DEPTH
    attn_w = w_attn_out.shape[1]
    conv_w = w_conv_out.shape[1]
    n_heads = attn_w // HEAD_DIM
    c0, c1, c2 = 3 * attn_w, 3 * attn_w + conv_w, 3 * attn_w + 2 * conv_w
    row = lambda v: v.reshape(1, -1)

    outs = []
    for bi in range(batch):
        h = x[bi]
        for l in range(depth):
            h, hb = _ffn_ln(h, ffn1_wg[l].astype(bf16), ffn1_wu[l].astype(bf16), ffn1_wd[l].astype(bf16),
                            row(ln1_g[l]), row(ln1_b[l]))
            w = w_in[l]
            qkv = _proj(hb, w[:, :c0].astype(bf16), sigmoid=False, name="qkv_proj")
            glu = _glu_proj(hb, w[:, c0:c1].astype(bf16), w[:, c1:c2].astype(bf16))
            gates = _proj(hb, w[:, c2:].astype(bf16), sigmoid=True, name="gate_proj")
            o = _moba_attention(qkv, n_heads)
            hc = _conv_ln(glu, conv_dw[l].reshape(CONV_KERNEL, conv_w), row(conv_db[l]),
                          row(conv_ln_g[l]), row(conv_ln_b[l]))
            merged = _merge(o, hc, gates, w_attn_out[l].astype(bf16), w_conv_out[l].astype(bf16))
            h = _out_ln(merged, h, w_out[l].astype(bf16), row(ln2_g[l]), row(ln2_b[l]))
            h, _ = _ffn_ln(h, ffn2_wg[l].astype(bf16), ffn2_wu[l].astype(bf16), ffn2_wd[l].astype(bf16),
                           row(ln3_g[l]), row(ln3_b[l]))
        outs.append(h)
    return outs[0][None] if batch == 1 else jnp.stack(outs, axis=0)
```

```python
import functools

import numpy as np
import jax
import jax.numpy as jnp
from jax import lax
from jax.experimental import pallas as pl
from jax.experimental.pallas import tpu as pltpu

HEAD_DIM = 128
MOBA_BLOCK = 256
MOBA_TOPK = 3
MOBA_CHUNK = 8
MOBA_PAIR = 4
EXP2_ZERO = 160.0
CONV_KERNEL = 31
LN_EPS = 1e-5
DEPTH = 1
DN_ALPHA = (2 * DEPTH) ** 0.25
ALIBI_MAX_LOG2 = 8.0
LOG2_E = 1.4426950408889634

V7X_VMEM_BYTES = 64 * 1024 * 1024
V7X_LANES = 128
V7X_SUBLANES = 8
CONV_HALO = 32
MASK_NEG = -1e30

f32 = jnp.float32
bf16 = jnp.bfloat16


def _vmem_limit(block_bytes):
    return int(min(V7X_VMEM_BYTES - 4 * 1024 * 1024, block_bytes + 16 * 1024 * 1024))


def _tile(dim, pref):
    if dim <= pref:
        return dim
    for t in range(pref, 0, -V7X_LANES):
        if dim % t == 0:
            return t
    raise ValueError((dim, pref))


def _layer_norm(y, g, b):
    mu = jnp.mean(y, axis=-1, keepdims=True)
    yc = y - mu
    var = jnp.mean(yc * yc, axis=-1, keepdims=True)
    return yc * lax.rsqrt(var + LN_EPS) * g + b


def _ffn_ln_kernel(x_ref, wg_ref, wu_ref, wd_ref, g_ref, b_ref, o_ref, ob_ref, xb_ref, *, ln_rows):
    f = pl.program_id(1)

    @pl.when(f == 0)
    def _():
        xb_ref[...] = x_ref[...].astype(bf16)
        o_ref[...] = jnp.zeros_like(o_ref)

    xb = xb_ref[...]
    gate = jnp.dot(xb, wg_ref[...].astype(bf16), preferred_element_type=f32)
    up = jnp.dot(xb, wu_ref[...].astype(bf16), preferred_element_type=f32)
    act = (gate * jax.nn.sigmoid(gate) * up).astype(bf16)
    o_ref[...] += jnp.dot(act, wd_ref[...].astype(bf16), preferred_element_type=f32)

    @pl.when(f == pl.num_programs(1) - 1)
    def _():
        def body(r, carry):
            rows = pl.ds(pl.multiple_of(r * ln_rows, ln_rows), ln_rows)
            y = DN_ALPHA * x_ref[rows, :] + 0.5 * o_ref[rows, :]
            out = _layer_norm(y, g_ref[...], b_ref[...])
            o_ref[rows, :] = out
            ob_ref[rows, :] = out.astype(bf16)
            return carry

        lax.fori_loop(0, o_ref.shape[0] // ln_rows, body, 0)


def _ffn_ln(x, wg, wu, wd, g, b):
    s, d = x.shape
    ff = wg.shape[1]
    tm = _tile(s, 1024)
    tf = _tile(ff, 256)
    wbytes = wg.dtype.itemsize
    blocks = 2 * (tm * d * 4 + 3 * d * tf * wbytes + tm * d * 4 + tm * d * 2)
    scratch = tm * d * 2
    return pl.pallas_call(
        functools.partial(_ffn_ln_kernel, ln_rows=_tile(tm, 256)),
        grid=(s // tm, ff // tf),
        in_specs=[
            pl.BlockSpec((tm, d), lambda i, j: (i, 0)),
            pl.BlockSpec((d, tf), lambda i, j: (0, j)),
            pl.BlockSpec((d, tf), lambda i, j: (0, j)),
            pl.BlockSpec((tf, d), lambda i, j: (j, 0)),
            pl.BlockSpec((1, d), lambda i, j: (0, 0)),
            pl.BlockSpec((1, d), lambda i, j: (0, 0)),
        ],
        out_specs=[
            pl.BlockSpec((tm, d), lambda i, j: (i, 0)),
            pl.BlockSpec((tm, d), lambda i, j: (i, 0)),
        ],
        out_shape=[jax.ShapeDtypeStruct((s, d), f32), jax.ShapeDtypeStruct((s, d), bf16)],
        scratch_shapes=[pltpu.VMEM((tm, d), bf16)],
        compiler_params=pltpu.CompilerParams(
            dimension_semantics=("arbitrary", "arbitrary"),
            vmem_limit_bytes=_vmem_limit(blocks + scratch)),
        name="ffn_ln",
    )(x, wg, wu, wd, g, b)


def _proj_kernel(x_ref, w_ref, o_ref, *, sigmoid):
    y = jnp.dot(x_ref[...], w_ref[...], preferred_element_type=f32)
    if sigmoid:
        y = jax.nn.sigmoid(y)
    o_ref[...] = y.astype(o_ref.dtype)


def _proj(x, w, *, sigmoid, name):
    s, d = x.shape
    n = w.shape[1]
    tm = _tile(s, 1024)
    tn = _tile(n, 1024)
    blocks = 2 * (tm * d * 2 + d * tn * 2 + tm * tn * 2)
    return pl.pallas_call(
        functools.partial(_proj_kernel, sigmoid=sigmoid),
        grid=(s // tm, n // tn),
        in_specs=[
            pl.BlockSpec((tm, d), lambda i, j: (i, 0)),
            pl.BlockSpec((d, tn), lambda i, j: (0, j)),
        ],
        out_specs=pl.BlockSpec((tm, tn), lambda i, j: (i, j)),
        out_shape=jax.ShapeDtypeStruct((s, n), bf16),
        compiler_params=pltpu.CompilerParams(
            dimension_semantics=("arbitrary", "arbitrary"),
            vmem_limit_bytes=_vmem_limit(blocks)),
        name=name,
    )(x, w)


def _glu_proj_kernel(x_ref, wa_ref, wg_ref, o_ref):
    x = x_ref[...]
    a = jnp.dot(x, wa_ref[...], preferred_element_type=f32)
    g = jnp.dot(x, wg_ref[...], preferred_element_type=f32)
    o_ref[...] = a * jax.nn.sigmoid(g)


def _glu_proj(x, wa, wg):
    s, d = x.shape
    n = wa.shape[1]
    tm = _tile(s, 1024)
    tn = _tile(n, 1024)
    blocks = 2 * (tm * d * 2 + 2 * d * tn * 2 + tm * tn * 4)
    return pl.pallas_call(
        _glu_proj_kernel,
        grid=(s // tm, n // tn),
        in_specs=[
            pl.BlockSpec((tm, d), lambda i, j: (i, 0)),
            pl.BlockSpec((d, tn), lambda i, j: (0, j)),
            pl.BlockSpec((d, tn), lambda i, j: (0, j)),
        ],
        out_specs=pl.BlockSpec((tm, tn), lambda i, j: (i, j)),
        out_shape=jax.ShapeDtypeStruct((s, n), f32),
        compiler_params=pltpu.CompilerParams(
            dimension_semantics=("arbitrary", "arbitrary"),
            vmem_limit_bytes=_vmem_limit(blocks)),
        name="glu_proj",
    )(x, wa, wg)


_NT = (((1,), (1,)), ((), ()))


def _moba_kernel(q_ref, qn_ref, k_ref, v_ref, slope_ref, o_ref, kaug_ref, kmean_ref, kmax_ref, qaug_ref, qnext_ref, s_ref, *, nb, chunk, scale):
    blk = MOBA_BLOCK
    span = chunk * blk
    pad = chunk - 1
    step = pl.program_id(1)
    qbs = [step + u * (nb // MOBA_PAIR) for u in range(MOBA_PAIR)]

    def make_qaug(q, n_past):
        gate = lax.dot_general(kmean_ref[...].astype(bf16), q, _NT, preferred_element_type=f32)
        row = lax.broadcasted_iota(jnp.int32, (V7X_LANES, blk), 0)
        neg_inf = f32(-jnp.inf)
        gate = jnp.where(row < n_past, gate, neg_inf)
        keep = jnp.zeros((V7X_LANES, blk), jnp.bool_)
        for r in range(min(MOBA_TOPK, nb)):
            top = jnp.max(gate, axis=0, keepdims=True)
            idx = jnp.min(jnp.where(gate == top, row, V7X_LANES), axis=0, keepdims=True)
            hit = row == idx
            keep = jnp.logical_or(keep, jnp.logical_and(hit, r < n_past))
            gate = jnp.where(hit, neg_inf, gate)
        maskbias = jnp.where(keep, 0.0, MASK_NEG).T.astype(bf16)
        return jnp.concatenate([(q.astype(f32) * scale).astype(bf16), maskbias], axis=1)

    @pl.when(step == 0)
    def _build_head():
        lane = lax.broadcasted_iota(jnp.int32, (blk, V7X_LANES), 1)
        kmean_ref[...] = jnp.zeros_like(kmean_ref)
        dead = jnp.where(lane == V7X_LANES - 1, 1.0, 0.0).astype(bf16)
        for j in range(pad):
            kaug_ref[j * blk:(j + 1) * blk, :] = jnp.concatenate([jnp.zeros((blk, HEAD_DIM), bf16), dead], axis=1)

        def body(j, kmax):
            kj = k_ref[pl.ds(pl.multiple_of(j * blk, blk), blk), :]
            onehot = jnp.where(lane == j, 1.0, 0.0).astype(bf16)
            kaug_ref[pl.ds(pl.multiple_of((j + pad) * blk, blk), blk), :] = jnp.concatenate([kj, onehot], axis=1)
            kf = kj.astype(f32)
            kmean_ref[pl.ds(j, 1), :] = jnp.mean(kf, axis=0, keepdims=True)
            k2 = jnp.max(jnp.sum(kf * kf, axis=1, keepdims=True), axis=0, keepdims=True)
            return jnp.maximum(kmax, k2)

        kmax_ref[...] = jnp.broadcast_to(lax.fori_loop(0, nb, body, jnp.zeros((1, 1), f32)), kmax_ref.shape)
        for u in range(MOBA_PAIR):
            qnext_ref[u] = make_qaug(q_ref[u], qbs[u])

    qaug_ref[...] = qnext_ref[...]
    for u in range(MOBA_PAIR):
        qnext_ref[u] = make_qaug(qn_ref[u], qbs[u] + 1)

    slope = slope_ref[0]
    cpos = lax.broadcasted_iota(jnp.int32, (1, blk), 1).astype(f32)
    half = blk // 2

    def lane_fold(x, op):
        return op(x[:, :half], x[:, half:])

    def first_block(u, c):
        return qbs[u] - chunk * (c + 1)

    def score_pass(u, c):
        b0 = jnp.maximum(first_block(u, c) + pad, 0)
        mx = None
        for t in range(chunk):
            r0 = pl.multiple_of((b0 + t) * blk, blk)
            st = lax.dot_general(qaug_ref[u], kaug_ref[pl.ds(r0, blk), :], _NT,
                                 preferred_element_type=f32)
            dist = (chunk * (c + 1) - t) * blk
            st = st + slope * (cpos - jnp.asarray(dist, f32))
            s_ref[u, :, t * blk:(t + 1) * blk] = st
            e = lane_fold(st, jnp.maximum)
            mx = e if mx is None else jnp.maximum(mx, e)
        return jnp.max(mx, axis=1, keepdims=True)

    def value_pass(u, c, m_new):
        ls = jnp.zeros((blk, half), f32)
        pv = jnp.zeros((blk, HEAD_DIM), f32)
        for t in range(chunk):
            r0 = pl.multiple_of(jnp.maximum(first_block(u, c) + t, 0) * blk, blk)
            pt = jnp.exp2(s_ref[u, :, t * blk:(t + 1) * blk] - m_new)
            ls = ls + lane_fold(pt, jnp.add)
            pv = pv + jnp.dot(pt.astype(bf16), v_ref[pl.ds(r0, blk), :], preferred_element_type=f32)
        return ls, pv

    def prologue(u):
        qb = qbs[u]
        qs = (q_ref[u].astype(f32) * scale).astype(bf16)
        qf = qs.astype(f32)
        q2 = jnp.max(jnp.sum(qf * qf, axis=1, keepdims=True), axis=0, keepdims=True)
        smax = jnp.sqrt(q2 * kmax_ref[0:1, 0:1])
        live = jnp.floor((2.0 * smax + EXP2_ZERO) / (slope[:, 0:1] * span)) + 1.0
        live = jnp.max(jnp.minimum(live, float(nb)).astype(jnp.int32))
        n_chunks = jnp.minimum(lax.div(qb + (chunk - 1), chunk), live)
        d0 = pl.multiple_of(qb * blk, blk)
        s = lax.dot_general(qs, k_ref[pl.ds(d0, blk), :], _NT, preferred_element_type=f32)
        s = s + slope * cpos
        rr = lax.broadcasted_iota(jnp.int32, (blk, blk), 0)
        cc = lax.broadcasted_iota(jnp.int32, (blk, blk), 1)
        s = jnp.where(cc <= rr, s, MASK_NEG)
        m0 = jnp.max(s, axis=1, keepdims=True)
        p = jnp.exp2(s - m0)
        l0 = lane_fold(p, jnp.add)
        acc0 = jnp.dot(p.astype(bf16), v_ref[pl.ds(d0, blk), :], preferred_element_type=f32)
        return n_chunks, (m0, l0, acc0, score_pass(u, 0))

    def chunk_loop(u, n_chunks, init):
        def body(c, carry):
            m, l, acc, mx_c = carry
            m_new = jnp.maximum(m, mx_c)
            alpha = jnp.exp2(m - m_new)
            ls, pv = value_pass(u, c, m_new)
            mx_next = score_pass(u, c + 1)
            return m_new, alpha * l + ls, alpha * acc + pv, mx_next

        return lax.fori_loop(0, n_chunks - 1, body, init)

    def last_pass(u, n_chunks, state):
        m, l, acc, mx_c = state
        some = n_chunks > 0
        m_new = jnp.maximum(m, jnp.where(some, mx_c, m))
        alpha = jnp.exp2(m - m_new)
        ls, pv = value_pass(u, jnp.maximum(n_chunks - 1, 0), jnp.where(some, m_new, -MASK_NEG))
        l = alpha * l + ls
        acc = alpha * acc + pv
        o_ref[u] = (acc / jnp.sum(l, axis=1, keepdims=True)).astype(o_ref.dtype)

    starts = [prologue(u) for u in range(MOBA_PAIR)]
    states = [chunk_loop(u, n, init) for u, (n, init) in enumerate(starts)]
    for u in range(MOBA_PAIR):
        last_pass(u, starts[u][0], states[u])


def _moba_attention(qkv, n_heads):
    s = qkv.shape[0]
    blk = MOBA_BLOCK
    assert s % blk == 0
    nb = s // blk
    assert nb < V7X_LANES
    chunk = min(MOBA_CHUNK, nb)
    span = chunk * blk
    kaug_rows = (nb + chunk - 1) * blk
    slopes = np.array([2.0 ** (-ALIBI_MAX_LOG2 * (h + 1) / n_heads) for h in range(n_heads)], np.float32)
    slopes = (slopes.astype(np.float64) * LOG2_E).astype(np.float32)
    slope_rows = jnp.asarray(np.broadcast_to(slopes[:, None, None], (n_heads, 1, blk)).copy())
    pair = MOBA_PAIR
    assert nb % pair == 0
    steps = nb // pair
    blocks = 2 * (2 * pair * blk * HEAD_DIM * 2 + 2 * s * HEAD_DIM * 2 + blk * 4 + pair * blk * HEAD_DIM * 2)
    scratch = (kaug_rows * 2 * HEAD_DIM * 2 + V7X_LANES * HEAD_DIM * 4 + V7X_SUBLANES * V7X_LANES * 4
               + 2 * pair * blk * 2 * HEAD_DIM * 2 + pair * blk * span * 4)
    q_view = qkv.reshape(pair, s // pair, qkv.shape[1])
    out = pl.pallas_call(
        functools.partial(_moba_kernel, nb=nb, chunk=chunk, scale=HEAD_DIM ** -0.5 * LOG2_E),
        grid=(n_heads, steps),
        in_specs=[
            pl.BlockSpec((pair, blk, HEAD_DIM), lambda h, i: (0, i, h)),
            pl.BlockSpec((pair, blk, HEAD_DIM), lambda h, i: (0, jnp.minimum(i + 1, steps - 1), h)),
            pl.BlockSpec((s, HEAD_DIM), lambda h, i: (0, n_heads + h)),
            pl.BlockSpec((s, HEAD_DIM), lambda h, i: (0, 2 * n_heads + h)),
            pl.BlockSpec((1, 1, blk), lambda h, i: (h, 0, 0)),
        ],
        out_specs=pl.BlockSpec((pair, blk, HEAD_DIM), lambda h, i: (0, i, h)),
        out_shape=jax.ShapeDtypeStruct((pair, s // pair, n_heads * HEAD_DIM), bf16),
        scratch_shapes=[pltpu.VMEM((kaug_rows, 2 * HEAD_DIM), bf16), pltpu.VMEM((V7X_LANES, HEAD_DIM), f32),
                        pltpu.VMEM((V7X_SUBLANES, V7X_LANES), f32),
                        pltpu.VMEM((pair, blk, 2 * HEAD_DIM), bf16), pltpu.VMEM((pair, blk, 2 * HEAD_DIM), bf16),
                        pltpu.VMEM((pair, blk, span), f32)],
        compiler_params=pltpu.CompilerParams(
            dimension_semantics=("arbitrary", "arbitrary"),
            vmem_limit_bytes=_vmem_limit(blocks + scratch)),
        name="moba_attn",
    )(q_view, q_view, qkv, qkv, slope_rows)
    return out.reshape(s, n_heads * HEAD_DIM)


def _conv_ln_kernel(prev_ref, cur_ref, w_ref, cb_ref, g_ref, b_ref, o_ref, ext_ref, y_ref, *, tm, lane_chunk, row_chunk):
    i = pl.program_id(0)
    c = cur_ref.shape[1]
    ext_ref[0:CONV_HALO, :] = jnp.where(i > 0, prev_ref[...], 0.0)
    ext_ref[CONV_HALO:CONV_HALO + tm, :] = cur_ref[...]
    first = CONV_HALO - (CONV_KERNEL - 1)
    sub = V7X_SUBLANES
    for r0 in range(0, tm, row_chunk):
        for c0 in range(0, c, lane_chunk):
            acc = jnp.zeros((row_chunk, lane_chunk), f32)
            for rho in range(sub):
                offs = [o for o in range(first, first + CONV_KERNEL) if o % sub == rho]
                lo, hi = offs[0] - rho, offs[-1] - rho
                win = ext_ref[r0 + rho + lo:r0 + rho + hi + row_chunk, c0:c0 + lane_chunk]
                for o in offs:
                    k = o - first
                    wk = jnp.tile(w_ref[k, :, c0:c0 + lane_chunk], (row_chunk // sub, 1))
                    acc = acc + wk * win[o - rho - lo:o - rho - lo + row_chunk]
            y_ref[r0:r0 + row_chunk, c0:c0 + lane_chunk] = acc + cb_ref[:, c0:c0 + lane_chunk]
    y = _layer_norm(y_ref[...], g_ref[...], b_ref[...])
    o_ref[...] = (y * jax.nn.sigmoid(y)).astype(o_ref.dtype)


def _conv_ln(hc, w, cb, g, b):
    s, c = hc.shape
    tm = _tile(s, 256)
    assert tm % CONV_HALO == 0 and CONV_HALO >= CONV_KERNEL - 1
    per = tm // CONV_HALO
    w = jnp.broadcast_to(w[:, None, :], (CONV_KERNEL, V7X_SUBLANES, c))
    blocks = 2 * (CONV_HALO * c * 4 + tm * c * 4 + CONV_KERNEL * V7X_SUBLANES * c * 4 + tm * c * 2)
    scratch = (tm + CONV_HALO) * c * 4 + tm * c * 4
    return pl.pallas_call(
        functools.partial(_conv_ln_kernel, tm=tm, lane_chunk=_tile(c, 256), row_chunk=_tile(tm, 64)),
        grid=(s // tm,),
        in_specs=[
            pl.BlockSpec((CONV_HALO, c), lambda i: (jnp.maximum(i * per - 1, 0), 0)),
            pl.BlockSpec((tm, c), lambda i: (i, 0)),
            pl.BlockSpec((CONV_KERNEL, V7X_SUBLANES, c), lambda i: (0, 0, 0)),
            pl.BlockSpec((1, c), lambda i: (0, 0)),
            pl.BlockSpec((1, c), lambda i: (0, 0)),
            pl.BlockSpec((1, c), lambda i: (0, 0)),
        ],
        out_specs=pl.BlockSpec((tm, c), lambda i: (i, 0)),
        out_shape=jax.ShapeDtypeStruct((s, c), bf16),
        scratch_shapes=[pltpu.VMEM((tm + CONV_HALO, c), f32), pltpu.VMEM((tm, c), f32)],
        compiler_params=pltpu.CompilerParams(
            dimension_semantics=("arbitrary",),
            vmem_limit_bytes=_vmem_limit(blocks + scratch)),
        name="conv_ln",
    )(hc, hc, w, cb, g, b)


def _merge_kernel(o_ref, hc_ref, ga_ref, gc_ref, wa_ref, wc_ref, m_ref):
    ya = jnp.dot(o_ref[...], wa_ref[...], preferred_element_type=f32)
    yc = jnp.dot(hc_ref[...], wc_ref[...], preferred_element_type=f32)
    m_ref[...] = (ga_ref[...].astype(f32) * ya + gc_ref[...].astype(f32) * yc).astype(m_ref.dtype)


def _merge(o, hc, gates, wa, wc):
    s, da = o.shape
    dc = hc.shape[1]
    d = wa.shape[1]
    tm = _tile(s, 1024)
    tn = _tile(d, 1024)
    nj = d // tn
    blocks = 2 * (tm * da * 2 + tm * dc * 2 + 2 * tm * tn * 2 + da * tn * 2 + dc * tn * 2 + tm * tn * 2)
    return pl.pallas_call(
        _merge_kernel,
        grid=(s // tm, nj),
        in_specs=[
            pl.BlockSpec((tm, da), lambda i, j: (i, 0)),
            pl.BlockSpec((tm, dc), lambda i, j: (i, 0)),
            pl.BlockSpec((tm, tn), lambda i, j: (i, j)),
            pl.BlockSpec((tm, tn), lambda i, j: (i, nj + j)),
            pl.BlockSpec((da, tn), lambda i, j: (0, j)),
            pl.BlockSpec((dc, tn), lambda i, j: (0, j)),
        ],
        out_specs=pl.BlockSpec((tm, tn), lambda i, j: (i, j)),
        out_shape=jax.ShapeDtypeStruct((s, d), bf16),
        compiler_params=pltpu.CompilerParams(
            dimension_semantics=("arbitrary", "arbitrary"),
            vmem_limit_bytes=_vmem_limit(blocks)),
        name="merge",
    )(o, hc, gates, gates, wa, wc)


def _out_ln_kernel(m_ref, h_ref, w_ref, g_ref, b_ref, o_ref):
    mix = jnp.dot(m_ref[...], w_ref[...], preferred_element_type=f32)
    o_ref[...] = _layer_norm(DN_ALPHA * h_ref[...] + mix, g_ref[...], b_ref[...])


def _out_ln(m, h, w, g, b):
    s, d = h.shape
    tm = _tile(s, 512)
    blocks = 2 * (tm * d * 2 + tm * d * 4 + d * d * 2 + tm * d * 4)
    return pl.pallas_call(
        _out_ln_kernel,
        grid=(s // tm,),
        in_specs=[
            pl.BlockSpec((tm, d), lambda i: (i, 0)),
            pl.BlockSpec((tm, d), lambda i: (i, 0)),
            pl.BlockSpec((d, d), lambda i: (0, 0)),
            pl.BlockSpec((1, d), lambda i: (0, 0)),
            pl.BlockSpec((1, d), lambda i: (0, 0)),
        ],
        out_specs=pl.BlockSpec((tm, d), lambda i: (i, 0)),
        out_shape=jax.ShapeDtypeStruct((s, d), f32),
        compiler_params=pltpu.CompilerParams(
            dimension_semantics=("arbitrary",),
            vmem_limit_bytes=_vmem_limit(blocks)),
        name="out_ln",
    )(m, h, w, g, b)


def kernel(x, ln1_g, ln1_b, ffn1_wg, ffn1_wu, ffn1_wd, w_in, conv_dw, conv_db, conv_ln_g, conv_ln_b, w_conv_out, w_attn_out, w_out, ln2_g, ln2_b, ffn2_wg, ffn2_wu, ffn2_wd, ln3_g, ln3_b):
    batch, seq, d_model = x.shape
    depth = ffn1_wg.shape[0]
    assert depth == DEPTH
    attn_w = w_attn_out.shape[1]
    conv_w = w_conv_out.shape[1]
    n_heads = attn_w // HEAD_DIM
    c0, c1, c2 = 3 * attn_w, 3 * attn_w + conv_w, 3 * attn_w + 2 * conv_w
    row = lambda v: v.reshape(1, -1)

    outs = []
    for bi in range(batch):
        h = x[bi]
        for l in range(depth):
            h, hb = _ffn_ln(h, ffn1_wg[l], ffn1_wu[l], ffn1_wd[l], row(ln1_g[l]), row(ln1_b[l]))
            w = w_in[l]
            qkv = _proj(hb, w[:, :c0].astype(bf16), sigmoid=False, name="qkv_proj")
            glu = _glu_proj(hb, w[:, c0:c1].astype(bf16), w[:, c1:c2].astype(bf16))
            gates = _proj(hb, w[:, c2:].astype(bf16), sigmoid=True, name="gate_proj")
            o = _moba_attention(qkv, n_heads)
            hc = _conv_ln(glu, conv_dw[l].reshape(CONV_KERNEL, conv_w), row(conv_db[l]),
                          row(conv_ln_g[l]), row(conv_ln_b[l]))
            merged = _merge(o, hc, gates, w_attn_out[l].astype(bf16), w_conv_out[l].astype(bf16))
            h = _out_ln(merged, h, w_out[l].astype(bf16), row(ln2_g[l]), row(ln2_b[l]))
            h, _ = _ffn_ln(h, ffn2_wg[l], ffn2_wu[l], ffn2_wd[l], row(ln3_g[l]), row(ln3_b[l]))
        outs.append(h)
    return outs[0][None] if batch == 1 else jnp.stack(outs, axis=0)
```
